```python
import math
import jax, jax.numpy as jnp
from jax import lax
import numpy as np

D_MODEL = 1024
BATCH = 8
SEQ = 2048
DEPTH = 4
DEC_BATCH = 128
DEC_SEQ = 4
PAST_LEN = 16384
PAGE_SIZE = 128

N_MIXERS = 4
D_PLE = 256
D_FF = 2816
EPS = 1e-6

GM_CHUNK = 128
GM_HALF = D_MODEL
GM_GROUPS = 8
GM_GROUP_DIM = GM_HALF // GM_GROUPS

POOL_WINDOWS = (2, 4, 8, 16)
POOL_GROUPS = len(POOL_WINDOWS)
POOL_GROUP_DIM = D_MODEL // POOL_GROUPS
POOL_HIST = max(POOL_WINDOWS) - 1

GLA_HEADS = 4
GLA_QK = D_MODEL // 2
GLA_V = D_MODEL
GLA_DK = GLA_QK // GLA_HEADS
GLA_DV = GLA_V // GLA_HEADS
GLA_RANK = 16
GLA_NORMALIZER = 16.0
GLA_CHUNK = 16

SSM_D_INNER = 2 * D_MODEL
SSM_HEAD_DIM = 64
SSM_HEADS = SSM_D_INNER // SSM_HEAD_DIM
SSM_GROUPS = 4
SSM_HPG = SSM_HEADS // SSM_GROUPS
SSM_STATE = 128
SSM_CONV = 4
SSM_CHUNK = 64
SSM_CONV_DIM = SSM_D_INNER + 2 * SSM_GROUPS * SSM_STATE

kernel_name = 'hybrid_macaron_chunkmlp_pool_gla_ssd_step'


def rms_norm(x, g):
    xf = x.astype(jnp.float32)
    y = xf * lax.rsqrt(jnp.mean(xf * xf, axis=-1, keepdims=True) + EPS)
    return (y * g.astype(jnp.float32)).astype(x.dtype)


def layer_norm(x, g):
    xf = x.astype(jnp.float32)
    xc = xf - jnp.mean(xf, axis=-1, keepdims=True)
    y = xc * lax.rsqrt(jnp.mean(xc * xc, axis=-1, keepdims=True) + EPS)
    return (y * g.astype(jnp.float32)).astype(x.dtype)


def swiglu(h, w_gate, w_up, w_down):
    return (jax.nn.silu(h @ w_gate) * (h @ w_up)) @ w_down


def pad_seq(t, multiple):
    pad = (-t.shape[1]) % multiple
    return jnp.pad(t, [(0, 0), (0, pad)] + [(0, 0)] * (t.ndim - 2))


def chunk_mlp_mixer(h, w_in, ln_g, w_s, b_s, w_out):
    B, L, _ = h.shape
    u, v = jnp.split(jax.nn.gelu(h @ w_in), 2, axis=-1)
    v = layer_norm(v, ln_g)
    vp = pad_seq(v, GM_CHUNK)
    nc = vp.shape[1] // GM_CHUNK
    vp = vp.reshape(B, nc, GM_CHUNK, GM_GROUPS, GM_GROUP_DIM)
    causal = jnp.tril(jnp.ones((GM_CHUNK, GM_CHUNK), dtype=bool))
    w_masked = jnp.where(causal[None], w_s, jnp.zeros_like(w_s))
    sv = jnp.einsum('gts,bnsgc->bntgc', w_masked, vp) + jnp.transpose(b_s)[None, None, :, :, None]
    sv = sv.reshape(B, nc * GM_CHUNK, GM_HALF)[:, :L]
    return (u * sv) @ w_out, v


def pool_mixer(h, hist, pos0, w_grp, scale):
    B, L, D = h.shape
    ext = jnp.concatenate([hist.astype(h.dtype), h], axis=1)
    cs = jnp.cumsum(ext.astype(jnp.float32), axis=1)
    cs = jnp.pad(cs, ((0, 0), (1, 0), (0, 0)))
    pos = pos0 + jnp.arange(L, dtype=jnp.int32)
    pooled = []
    for gi, w in enumerate(POOL_WINDOWS):
        lo, hi = gi * POOL_GROUP_DIM, (gi + 1) * POOL_GROUP_DIM
        win_sum = (cs[:, POOL_HIST + 1:POOL_HIST + 1 + L, lo:hi]
                   - cs[:, POOL_HIST + 1 - w:POOL_HIST + 1 - w + L, lo:hi])
        count = jnp.minimum(pos + 1, w).astype(jnp.float32)[None, :, None]
        pooled.append(win_sum / count)
    pooled = jnp.concatenate(pooled, axis=-1)
    diff = (pooled - h.astype(jnp.float32)).astype(h.dtype).reshape(B, L, POOL_GROUPS, POOL_GROUP_DIM)
    y = jnp.einsum('blgc,gcd->blgd', diff, w_grp).reshape(B, L, D) * scale
    return y, ext[:, -POOL_HIST:]


def gla_recurrence(q, k, v, log_a, s0):
    B, L, H, _ = q.shape
    DV = v.shape[-1]

    def blocks(t):
        t = pad_seq(t, GLA_CHUNK)
        nc = t.shape[1] // GLA_CHUNK
        return t.reshape(B, nc, GLA_CHUNK, H, t.shape[-1]).transpose(1, 0, 3, 2, 4)

    qc, kc, vc, gc = blocks(q), blocks(k), blocks(v), blocks(log_a)
    b = jnp.cumsum(gc, axis=3)
    b_last = b[:, :, :, -1:, :]
    q_dec = qc * jnp.exp(b)
    k_inv = kc * jnp.exp(-b)
    k_end = kc * jnp.exp(b_last - b)
    causal = jnp.tril(jnp.ones((GLA_CHUNK, GLA_CHUNK), dtype=bool))
    scores = jnp.where(causal, jnp.einsum('nbhtd,nbhsd->nbhts', q_dec, k_inv), 0.0)
    o_intra = jnp.einsum('nbhts,nbhsv->nbhtv', scores, vc)

    def step(s, xs):
        q_dec_c, k_end_c, v_c, decay_c = xs
        o_inter = jnp.einsum('bhtd,bhdv->bhtv', q_dec_c, s)
        s = s * decay_c[:, :, 0, :, None] + jnp.einsum('bhsd,bhsv->bhdv', k_end_c, v_c)
        return s, o_inter

    s_final, o_inter = lax.scan(step, s0, (q_dec, k_end, vc, jnp.exp(b_last)))
    o = (o_intra + o_inter).transpose(1, 0, 3, 2, 4).reshape(B, -1, H, DV)[:, :L]
    return o, s_final


def gla_mixer(h, s0, w_in, w_a1, w_a2, b_a, g_norm, w_out):
    B, L, _ = h.shape
    q, k, v, r = jnp.split(h @ w_in, [GLA_QK, 2 * GLA_QK, 2 * GLA_QK + GLA_V], axis=-1)
    log_a = jax.nn.log_sigmoid(((h @ w_a1) @ w_a2 + b_a).astype(jnp.float32)) / GLA_NORMALIZER
    qh = q.astype(jnp.float32).reshape(B, L, GLA_HEADS, GLA_DK) * (GLA_DK ** -0.5)
    kh = k.astype(jnp.float32).reshape(B, L, GLA_HEADS, GLA_DK)
    vh = v.astype(jnp.float32).reshape(B, L, GLA_HEADS, GLA_DV)
    o, s_new = gla_recurrence(qh, kh, vh, log_a.reshape(B, L, GLA_HEADS, GLA_DK), s0.astype(jnp.float32))
    o = rms_norm(o, g_norm.reshape(GLA_HEADS, GLA_DV)).astype(h.dtype).reshape(B, L, GLA_V)
    return (o * jax.nn.silu(r)) @ w_out, s_new.astype(s0.dtype)


def ssd_recurrence(x, dt, a, bm, cm, s0):
    B, L = x.shape[:2]

    def blocks(t):
        t = pad_seq(t, SSM_CHUNK)
        return t.reshape((B, t.shape[1] // SSM_CHUNK, SSM_CHUNK) + t.shape[2:])

    xc = blocks(x).reshape(B, -1, SSM_CHUNK, SSM_GROUPS, SSM_HPG, SSM_HEAD_DIM)
    dtc = blocks(dt).reshape(B, -1, SSM_CHUNK, SSM_GROUPS, SSM_HPG)
    bc, cc = blocks(bm), blocks(cm)
    cum = jnp.cumsum(dtc * a.reshape(SSM_GROUPS, SSM_HPG), axis=2)
    causal = jnp.tril(jnp.ones((SSM_CHUNK, SSM_CHUNK), dtype=bool))[:, :, None, None]
    seg = cum[:, :, :, None] - cum[:, :, None, :]
    decay = jnp.exp(jnp.where(causal, seg, -jnp.inf))
    cb = jnp.einsum('bctgn,bcsgn->bctsg', cc, bc)
    mix = cb[..., None] * decay * dtc[:, :, None]
    y_intra = jnp.einsum('bctsgr,bcsgrp->bctgrp', mix, xc)

    def step(s, xs):
        c_c, b_c, x_c, dt_c, cum_c = xs
        y_inter = jnp.einsum('btgn,bgrpn->btgrp', c_c, s) * jnp.exp(cum_c)[..., None]
        w_end = jnp.exp(cum_c[:, -1:] - cum_c) * dt_c
        s = s * jnp.exp(cum_c[:, -1])[..., None, None] + jnp.einsum('bsgn,bsgr,bsgrp->bgrpn', b_c, w_end, x_c)
        return s, y_inter

    to_scan = lambda t: jnp.moveaxis(t, 1, 0)
    s_init = s0.reshape(B, SSM_GROUPS, SSM_HPG, SSM_HEAD_DIM, SSM_STATE)
    s_final, y_inter = lax.scan(step, s_init, (to_scan(cc), to_scan(bc), to_scan(xc), to_scan(dtc), to_scan(cum)))
    y = (y_intra + jnp.moveaxis(y_inter, 0, 1)).reshape(B, -1, SSM_HEADS, SSM_HEAD_DIM)[:, :L]
    return y, s_final.reshape(B, SSM_HEADS, SSM_HEAD_DIM, SSM_STATE)


def ssd_mixer(h, conv_s0, ssm_s0, w_in, conv_w, conv_b, dt_bias, a_log, d_skip, g_norm, w_out):
    B, L, _ = h.shape
    z, xbc, dt = jnp.split(h @ w_in, [SSM_D_INNER, SSM_D_INNER + SSM_CONV_DIM], axis=-1)
    ext = jnp.concatenate([conv_s0.astype(h.dtype), xbc], axis=1)
    conv = conv_b + sum(ext[:, j:j + L] * conv_w[j] for j in range(SSM_CONV))
    xbc = jax.nn.silu(conv)
    xs, bm, cm = jnp.split(xbc, [SSM_D_INNER, SSM_D_INNER + SSM_GROUPS * SSM_STATE], axis=-1)
    dt = jax.nn.softplus((dt + dt_bias).astype(jnp.float32))
    a = -jnp.exp(a_log.astype(jnp.float32))
    xh = xs.astype(jnp.float32).reshape(B, L, SSM_HEADS, SSM_HEAD_DIM)
    y, s_new = ssd_recurrence(xh, dt, a,
                              bm.astype(jnp.float32).reshape(B, L, SSM_GROUPS, SSM_STATE),
                              cm.astype(jnp.float32).reshape(B, L, SSM_GROUPS, SSM_STATE),
                              ssm_s0.astype(jnp.float32))
    y = y + xh * d_skip.astype(jnp.float32)[:, None]
    y = y.reshape(B, L, SSM_D_INNER).astype(h.dtype) * jax.nn.silu(z)
    y = rms_norm(y.reshape(B, L, SSM_GROUPS, SSM_D_INNER // SSM_GROUPS),
                 g_norm.reshape(SSM_GROUPS, SSM_D_INNER // SSM_GROUPS)).reshape(B, L, SSM_D_INNER)
    return y @ w_out, ext[:, -(SSM_CONV - 1):], s_new.astype(ssm_s0.dtype)


def setup_inputs(seed: int = 0) -> dict:
    key = jax.random.key(seed)
    keys = iter(jax.random.split(key, 64))
    nk = lambda: next(keys)
    normal = lambda shape, s=1.0: jax.random.normal(nk(), shape, jnp.float32) * s
    dense = lambda shape, fan_in: normal(shape, fan_in ** -0.5)
    gain = lambda shape: 1.0 + normal(shape, 0.02)

    dt0 = jnp.exp(jax.random.uniform(nk(), (SSM_HEADS,), jnp.float32, math.log(1e-3), math.log(1e-1)))
    return {
        'x_prompt': normal((BATCH, SEQ, D_MODEL)),
        'x_sample': normal((DEC_BATCH, DEC_SEQ, D_MODEL)),
        'state_pool_l1': normal((DEC_BATCH, POOL_HIST, D_MODEL)),
        'state_gla_l2': normal((DEC_BATCH, GLA_HEADS, GLA_DK, GLA_DV), 0.5),
        'state_ssm_l3': normal((DEC_BATCH, SSM_HEADS, SSM_HEAD_DIM, SSM_STATE), 0.1),
        'state_conv_l3': normal((DEC_BATCH, SSM_CONV - 1, SSM_CONV_DIM)),
        'p_prompt': normal((DEPTH, BATCH, SEQ, D_PLE)),
        'p_sample': normal((DEPTH, DEC_BATCH, DEC_SEQ, D_PLE)),
        'norm_ffn1': gain((DEPTH, D_MODEL)),
        'ffn1_gate': dense((DEPTH, D_MODEL, D_FF), D_MODEL),
        'ffn1_up': dense((DEPTH, D_MODEL, D_FF), D_MODEL),
        'ffn1_down': dense((DEPTH, D_FF, D_MODEL), D_FF),
        'norm_mix': gain((DEPTH, D_MODEL)),
        'norm_ffn2': gain((DEPTH, D_MODEL)),
        'ffn2_gate': dense((DEPTH, D_MODEL, D_FF), D_MODEL),
        'ffn2_up': dense((DEPTH, D_MODEL, D_FF), D_MODEL),
        'ffn2_down': dense((DEPTH, D_FF, D_MODEL), D_FF),
        'norm_ple': gain((DEPTH, D_MODEL)),
        'ple_gate': dense((DEPTH, D_MODEL, D_MODEL), D_MODEL),
        'ple_proj': dense((DEPTH, D_PLE, D_MODEL), D_PLE),
        'norm_final': gain((D_MODEL,)),
        'gm_w_in': dense((D_MODEL, 2 * GM_HALF), D_MODEL),
        'gm_ln': gain((GM_HALF,)),
        'gm_w_s': dense((GM_GROUPS, GM_CHUNK, GM_CHUNK), GM_CHUNK),
        'gm_b_s': 1.0 + normal((GM_GROUPS, GM_CHUNK), 0.1),
        'gm_w_out': dense((GM_HALF, D_MODEL), GM_HALF),
        'pool_w': dense((POOL_GROUPS, POOL_GROUP_DIM, POOL_GROUP_DIM), POOL_GROUP_DIM),
        'pool_scale': gain((D_MODEL,)),
        'gla_w_in': dense((D_MODEL, 2 * GLA_QK + 2 * GLA_V), D_MODEL),
        'gla_w_a1': dense((D_MODEL, GLA_RANK), D_MODEL),
        'gla_w_a2': dense((GLA_RANK, GLA_QK), GLA_RANK),
        'gla_b_a': normal((GLA_QK,), 0.5),
        'gla_norm': gain((GLA_V,)),
        'gla_w_out': dense((GLA_V, D_MODEL), GLA_V),
        'ssm_w_in': dense((D_MODEL, 2 * SSM_D_INNER + 2 * SSM_GROUPS * SSM_STATE + SSM_HEADS), D_MODEL),
        'ssm_conv_w': dense((SSM_CONV, SSM_CONV_DIM), SSM_CONV),
        'ssm_conv_b': normal((SSM_CONV_DIM,), 0.02),
        'ssm_dt_bias': dt0 + jnp.log(-jnp.expm1(-dt0)),
        'ssm_a_log': jnp.log(jax.random.uniform(nk(), (SSM_HEADS,), jnp.float32, 1.0, 16.0)),
        'ssm_d': gain((SSM_HEADS,)),
        'ssm_norm': gain((SSM_D_INNER,)),
        'ssm_w_out': dense((SSM_D_INNER, D_MODEL), SSM_D_INNER),
    }


def reference(x_prompt, x_sample, state_pool_l1, state_gla_l2, state_ssm_l3, state_conv_l3,
              p_prompt, p_sample,
              norm_ffn1, ffn1_gate, ffn1_up, ffn1_down, norm_mix,
              norm_ffn2, ffn2_gate, ffn2_up, ffn2_down,
              norm_ple, ple_gate, ple_proj, norm_final,
              gm_w_in, gm_ln, gm_w_s, gm_b_s, gm_w_out,
              pool_w, pool_scale,
              gla_w_in, gla_w_a1, gla_w_a2, gla_b_a, gla_norm, gla_w_out,
              ssm_w_in, ssm_conv_w, ssm_conv_b, ssm_dt_bias, ssm_a_log, ssm_d, ssm_norm, ssm_w_out):

    def trunk(x, p, pos0, pool_hist, gla_s0, ssm_s0, conv_s0):
        for i in range(DEPTH):
            x = x + 0.5 * swiglu(rms_norm(x, norm_ffn1[i]), ffn1_gate[i], ffn1_up[i], ffn1_down[i])
            h = rms_norm(x, norm_mix[i])
            kind = i % N_MIXERS
            if kind == 0:
                mixed, chunk_v = chunk_mlp_mixer(h, gm_w_in, gm_ln, gm_w_s, gm_b_s, gm_w_out)
            elif kind == 1:
                mixed, pool_new = pool_mixer(h, pool_hist, pos0, pool_w, pool_scale)
            elif kind == 2:
                mixed, gla_new = gla_mixer(h, gla_s0, gla_w_in, gla_w_a1, gla_w_a2, gla_b_a, gla_norm, gla_w_out)
            else:
                mixed, conv_new, ssm_new = ssd_mixer(h, conv_s0, ssm_s0, ssm_w_in, ssm_conv_w, ssm_conv_b,
                                                     ssm_dt_bias, ssm_a_log, ssm_d, ssm_norm, ssm_w_out)
            x = x + mixed
            x = x + 0.5 * swiglu(rms_norm(x, norm_ffn2[i]), ffn2_gate[i], ffn2_up[i], ffn2_down[i])
            x = x + jax.nn.sigmoid(rms_norm(x, norm_ple[i]) @ ple_gate[i]) * (p[i] @ ple_proj[i])
        return rms_norm(x, norm_final), chunk_v, pool_new, gla_new, ssm_new, conv_new

    nb = x_prompt.shape[0]
    y_prompt, _, pool_prompt, gla_prompt, ssm_prompt, conv_prompt = trunk(
        x_prompt, p_prompt, 0,
        jnp.zeros((nb,) + state_pool_l1.shape[1:], state_pool_l1.dtype),
        jnp.zeros((nb,) + state_gla_l2.shape[1:], state_gla_l2.dtype),
        jnp.zeros((nb,) + state_ssm_l3.shape[1:], state_ssm_l3.dtype),
        jnp.zeros((nb,) + state_conv_l3.shape[1:], state_conv_l3.dtype))
    y_sample, chunk_v_sample, pool_sample, gla_sample, ssm_sample, conv_sample = trunk(
        x_sample, p_sample, PAST_LEN, state_pool_l1, state_gla_l2, state_ssm_l3, state_conv_l3)
    return (y_prompt, y_sample, chunk_v_sample, pool_prompt, pool_sample, gla_prompt, gla_sample,
            ssm_prompt, ssm_sample, conv_prompt, conv_sample)
```

```python
import functools

import jax
import jax.numpy as jnp
import numpy as np
from jax import lax
from jax.experimental import pallas as pl
from jax.experimental.pallas import tpu as pltpu

f32 = jnp.float32
bf16 = jnp.bfloat16

D = 1024
F = 2816
FC = 256
NF = F // FC
D_PLE = 256
EPS = 1e-6
DEPTH = 4

GM_CHUNK = 128
GM_GROUPS = 8
POOL_WINDOWS = (2, 4, 8, 16)
POOL_GD = D // 4
POOL_HIST = 15

GLA_H = 4
GLA_DK = 128
GLA_DV = 256
GLA_QK = GLA_H * GLA_DK
GLA_V = GLA_H * GLA_DV
GLA_NORMALIZER = 16.0
GLA_C = 64

SSM_DI = 2048
SSM_P = 64
SSM_H = 32
SSM_G = 4
SSM_HPG = 8
SSM_N = 128
SSM_CONV_DIM = SSM_DI + 2 * SSM_G * SSM_N
SSM_C = 128
GD = SSM_HPG * SSM_P

DEC_SEQ = 4
PAST_LEN = 16384

VMEM_LIMIT = 56 * 1024 * 1024


def _dot(a, b):
    return jnp.dot(a, b, preferred_element_type=f32)


def _dot_nt(a, b):
    return lax.dot_general(a, b, (((1,), (1,)), ((), ())), preferred_element_type=f32)


def _dot_tn(a, b):
    return lax.dot_general(a, b, (((0,), (0,)), ((), ())), preferred_element_type=f32)


def _dot_hi(a, b):
    return jnp.dot(a, b, precision=lax.Precision.HIGHEST, preferred_element_type=f32)


def _dot_hi_tn(a, b):
    return lax.dot_general(a, b, (((0,), (0,)), ((), ())), precision=lax.Precision.HIGHEST,
                           preferred_element_type=f32)


def _split3(x):
    hi = x.astype(bf16)
    r1 = x - hi.astype(f32)
    mid = r1.astype(bf16)
    lo = (r1 - mid.astype(f32)).astype(bf16)
    return hi, mid, lo


def _sel_dot(sel, parts):
    hi, mid, lo = parts
    return _dot(sel, hi) + _dot(sel, mid) + _dot(sel, lo)


def _rms(x, g):
    return x * lax.rsqrt(jnp.mean(x * x, axis=-1, keepdims=True) + EPS) * g


def _silu(x):
    return x * jax.nn.sigmoid(x)


def _cspec(shape):
    nd = len(shape)
    return pl.BlockSpec(shape, lambda *_: (0,) * nd, pipeline_mode=pl.Buffered(1))


def _params(*sem):
    return pltpu.CompilerParams(dimension_semantics=tuple(sem), vmem_limit_bytes=VMEM_LIMIT)


def _iota2(shape, dim):
    return lax.broadcasted_iota(jnp.int32, shape, dim)


def _ffn_kernel(*refs, ple, final):
    x_ref, g_ref, wg_ref, wu_ref, wd_ref = refs[:5]
    k = 5
    if ple:
        p_ref, gp_ref, wpg_ref, wpp_ref = refs[k:k + 4]
        k += 4
    if final:
        gf_ref = refs[k]
        k += 1
    o_ref, h_scr, acc_scr = refs[k:k + 3]

    x = x_ref[...]
    h_scr[...] = _rms(x, g_ref[...]).astype(bf16)
    for j in range(NF):
        sl = slice(j * FC, (j + 1) * FC)
        hb = h_scr[...]
        a = _silu(_dot(hb, wg_ref[:, sl])) * _dot(hb, wu_ref[:, sl])
        contrib = _dot(a.astype(bf16), wd_ref[sl, :])
        if j == 0:
            acc_scr[...] = contrib
        else:
            acc_scr[...] += contrib
    y = x + 0.5 * acc_scr[...]
    if ple:
        h2 = _rms(y, gp_ref[...]).astype(bf16)
        gate = jax.nn.sigmoid(_dot(h2, wpg_ref[...]))
        y = y + gate * _dot(p_ref[...].astype(bf16), wpp_ref[...])
    if final:
        y = _rms(y, gf_ref[...])
    o_ref[...] = y


def _ffn(x, g, wg, wu, wd, ple=None, final_g=None, tm=512):
    T = x.shape[0]
    row = pl.BlockSpec((tm, D), lambda i: (i, 0))
    in_specs = [row, _cspec((1, D)), _cspec((D, F)), _cspec((D, F)), _cspec((F, D))]
    args = [x, g.reshape(1, D), wg, wu, wd]
    if ple is not None:
        p2d, gp, wpg, wpp = ple
        in_specs += [pl.BlockSpec((tm, D_PLE), lambda i: (i, 0)), _cspec((1, D)), _cspec((D, D)),
                     _cspec((D_PLE, D))]
        args += [p2d, gp.reshape(1, D), wpg, wpp]
    if final_g is not None:
        in_specs.append(_cspec((1, D)))
        args.append(final_g.reshape(1, D))
    return pl.pallas_call(
        functools.partial(_ffn_kernel, ple=ple is not None, final=final_g is not None),
        out_shape=jax.ShapeDtypeStruct((T, D), f32),
        grid=(T // tm,),
        in_specs=in_specs,
        out_specs=row,
        scratch_shapes=[pltpu.VMEM((tm, D), bf16), pltpu.VMEM((tm, D), f32)],
        compiler_params=_params("arbitrary"),
        name="ffn",
    )(*args)


def _gmlp_pre(x, g, win_ref, ln_ref):
    h = _rms(x, g).astype(bf16)
    y = jax.nn.gelu(_dot(h, win_ref[...]), approximate=True)
    u = y[:, :D]
    v = y[:, D:]
    vc = v - jnp.mean(v, axis=-1, keepdims=True)
    v = vc * lax.rsqrt(jnp.mean(vc * vc, axis=-1, keepdims=True) + EPS) * ln_ref[...]
    return u, v


def _gmlp_prompt_kernel(x_ref, g_ref, win_ref, ln_ref, ws_ref, bst_ref, wout_ref, o_ref,
                        u_scr, vb_scr, t_scr):
    tm = x_ref.shape[0]
    x = x_ref[...]
    u, v = _gmlp_pre(x, g_ref[...], win_ref, ln_ref)
    u_scr[...] = u
    vb_scr[...] = v.astype(bf16)
    causal = _iota2((GM_CHUNK, GM_CHUNK), 0) >= _iota2((GM_CHUNK, GM_CHUNK), 1)
    for g in range(GM_GROUPS):
        wm = jnp.where(causal, ws_ref[g], 0.0).astype(bf16)
        bias = bst_ref[:, g:g + 1]
        cs = slice(g * 128, (g + 1) * 128)
        for ch in range(tm // GM_CHUNK):
            rs = slice(ch * GM_CHUNK, (ch + 1) * GM_CHUNK)
            sv = _dot(wm, vb_scr[rs, cs]) + bias
            t_scr[rs, cs] = (u_scr[rs, cs] * sv).astype(bf16)
    o_ref[...] = x + _dot(t_scr[...], wout_ref[...])


def _gmlp_prompt(x, g, win, ln, ws, bst, wout, tm=512):
    T = x.shape[0]
    row = pl.BlockSpec((tm, D), lambda i: (i, 0))
    return pl.pallas_call(
        _gmlp_prompt_kernel,
        out_shape=jax.ShapeDtypeStruct((T, D), f32),
        grid=(T // tm,),
        in_specs=[row, _cspec((1, D)), _cspec((D, 2 * D)), _cspec((1, D)),
                  _cspec((GM_GROUPS, GM_CHUNK, GM_CHUNK)), _cspec((GM_CHUNK, GM_GROUPS)),
                  _cspec((D, D))],
        out_specs=row,
        scratch_shapes=[pltpu.VMEM((tm, D), f32), pltpu.VMEM((tm, D), bf16), pltpu.VMEM((tm, D), bf16)],
        compiler_params=_params("arbitrary"),
        name="gmlp_prompt",
    )(x, g.reshape(1, D), win, ln.reshape(1, D), ws, bst, wout)


def _gmlp_sample_kernel(x_ref, g_ref, win_ref, ln_ref, wj_ref, bt_ref, wout_ref, o_ref, v_ref):
    rows = x_ref.shape[0]
    n = rows // 8
    x = x_ref[...]
    u, v = _gmlp_pre(x, g_ref[...], win_ref, ln_ref)
    v_ref[...] = v
    acc = v.reshape(n, 8, D) * wj_ref[0][None]
    for j in range(1, DEC_SEQ):
        acc = acc + pltpu.roll(v, j, axis=0).reshape(n, 8, D) * wj_ref[j][None]
    sv = (acc + bt_ref[...][None]).reshape(rows, D)
    o_ref[...] = x + _dot((u * sv).astype(bf16), wout_ref[...])


def _gmlp_sample(x, g, win, ln, wj, bt, wout):
    T = x.shape[0]
    full = pl.BlockSpec((T, D), lambda i: (0, 0))
    return pl.pallas_call(
        _gmlp_sample_kernel,
        out_shape=(jax.ShapeDtypeStruct((T, D), f32), jax.ShapeDtypeStruct((T, D), f32)),
        grid=(1,),
        in_specs=[full, _cspec((1, D)), _cspec((D, 2 * D)), _cspec((1, D)),
                  _cspec((DEC_SEQ, 8, D)), _cspec((8, D)), _cspec((D, D))],
        out_specs=(full, full),
        compiler_params=_params("arbitrary"),
        name="gmlp_sample",
    )(x, g.reshape(1, D), win, ln.reshape(1, D), wj, bt, wout)


def _pool_prompt_kernel(x_ref, g_ref, pw_ref, sc_ref, o_ref, pn_ref, ext_scr, *, nl):
    tm = x_ref.shape[0]
    l = pl.program_id(1)

    @pl.when(l == 0)
    def _():
        ext_scr[0:16, :] = jnp.zeros((16, D), f32)

    x = x_ref[...]
    ext_scr[16:16 + tm, :] = _rms(x, g_ref[...])
    pos = l * tm + _iota2((tm, 1), 0)
    for gi, w in enumerate(POOL_WINDOWS):
        cs = slice(gi * POOL_GD, (gi + 1) * POOL_GD)
        h = ext_scr[16:16 + tm, cs]
        s = h
        for j in range(1, w):
            s = s + ext_scr[16 - j:16 - j + tm, cs]
        cnt = jnp.minimum(pos + 1, w).astype(f32)
        diff = (s / cnt - h).astype(bf16)
        o_ref[:, cs] = x[:, cs] + _dot(diff, pw_ref[gi]) * sc_ref[:, cs]

    @pl.when(l == nl - 1)
    def _():
        pn_ref[0] = ext_scr[16 + tm - POOL_HIST:16 + tm, :]

    ext_scr[0:16, :] = ext_scr[tm:tm + 16, :]


def _pool_prompt(x, g, pw, sc, nb, tm=512):
    T = x.shape[0]
    nl = T // nb // tm
    row = pl.BlockSpec((tm, D), lambda b, l: (b * nl + l, 0))
    return pl.pallas_call(
        functools.partial(_pool_prompt_kernel, nl=nl),
        out_shape=(jax.ShapeDtypeStruct((T, D), f32), jax.ShapeDtypeStruct((nb, POOL_HIST, D), f32)),
        grid=(nb, nl),
        in_specs=[row, _cspec((1, D)), _cspec((4, POOL_GD, POOL_GD)), _cspec((1, D))],
        out_specs=(row, pl.BlockSpec((1, POOL_HIST, D), lambda b, l: (b, 0, 0))),
        scratch_shapes=[pltpu.VMEM((tm + 16, D), f32)],
        compiler_params=_params("arbitrary", "arbitrary"),
        name="pool_prompt",
    )(x, g.reshape(1, D), pw, sc.reshape(1, D))


_SB = 8


def _pool_sample_kernel(x_ref, hist_ref, g_ref, pw_ref, sc_ref, sel_ref, o_ref, h_ref, *, pos0):
    rows = x_ref.shape[0]
    x = x_ref[...]
    h = _rms(x, g_ref[...])
    h_ref[...] = h
    t = _iota2((rows, 1), 0) & (DEC_SEQ - 1)
    parts = _split3(hist_ref[...])
    for gi, w in enumerate(POOL_WINDOWS):
        cs = slice(gi * POOL_GD, (gi + 1) * POOL_GD)
        hc = h[:, cs]
        s = hc
        for j in range(1, min(w, DEC_SEQ)):
            s = s + jnp.where(t >= j, pltpu.roll(hc, j, axis=0), 0.0)
        s = s + _sel_dot(sel_ref[gi], tuple(p[:, cs] for p in parts))
        cnt = jnp.minimum(pos0 + t + 1, w).astype(f32)
        diff = (s / cnt - hc).astype(bf16)
        o_ref[:, cs] = x[:, cs] + _dot(diff, pw_ref[gi]) * sc_ref[:, cs]


def _pool_sample(x, hist2d, g, pw, sc, sel, pos0):
    T = x.shape[0]
    rb = _SB * DEC_SEQ
    hb = _SB * POOL_HIST
    row = pl.BlockSpec((rb, D), lambda i: (i, 0))
    return pl.pallas_call(
        functools.partial(_pool_sample_kernel, pos0=pos0),
        out_shape=(jax.ShapeDtypeStruct((T, D), f32), jax.ShapeDtypeStruct((T, D), f32)),
        grid=(T // rb,),
        in_specs=[row, pl.BlockSpec((hb, D), lambda i: (i, 0)), _cspec((1, D)),
                  _cspec((4, POOL_GD, POOL_GD)), _cspec((1, D)), _cspec((4, rb, hb))],
        out_specs=(row, row),
        compiler_params=_params("arbitrary"),
        name="pool_sample",
    )(x, hist2d, g.reshape(1, D), pw, sc.reshape(1, D), sel)


def _gla_pre_kernel(x_ref, g_ref, win_ref, wa1_ref, wa2_ref, ba_ref, qkvr_ref, la_ref):
    h = _rms(x_ref[...], g_ref[...]).astype(bf16)
    qkvr_ref[...] = _dot(h, win_ref[...])
    a1 = _dot(h, wa1_ref[...]).astype(bf16)
    z = _dot(a1, wa2_ref[...]) + ba_ref[...]
    la_ref[...] = jax.nn.log_sigmoid(z) / GLA_NORMALIZER


def _gla_pre(x, g, win, wa1, wa2, ba, tm=512):
    T = x.shape[0]
    W = 2 * GLA_QK + 2 * GLA_V
    return pl.pallas_call(
        _gla_pre_kernel,
        out_shape=(jax.ShapeDtypeStruct((T, W), f32), jax.ShapeDtypeStruct((T, GLA_QK), f32)),
        grid=(T // tm,),
        in_specs=[pl.BlockSpec((tm, D), lambda i: (i, 0)), _cspec((1, D)), _cspec((D, W)),
                  _cspec((D, 16)), _cspec((16, GLA_QK)), _cspec((1, GLA_QK))],
        out_specs=(pl.BlockSpec((tm, W), lambda i: (i, 0)), pl.BlockSpec((tm, GLA_QK), lambda i: (i, 0))),
        compiler_params=_params("arbitrary"),
        name="gla_pre",
    )(x, g.reshape(1, D), win, wa1, wa2, ba.reshape(1, GLA_QK))


def _gla_chunk(q, k, v, b, bl, anc, mask, segs, C):
    qs = q * (GLA_DK ** -0.5)
    q_dec = (qs * jnp.exp(b)).astype(bf16)
    k_end = k * jnp.exp(bl - b)
    q_mid = (qs * jnp.exp(b - anc)).astype(bf16)
    k_mid = (k * jnp.exp(anc - b)).astype(bf16)
    ebl = jnp.exp(bl)
    vb = v.astype(bf16)
    rid = _iota2((C, 1), 0)
    outs = []
    for hd in range(GLA_H):
        ds_ = slice(hd * GLA_DK, (hd + 1) * GLA_DK)
        vs = slice(hd * GLA_DV, (hd + 1) * GLA_DV)
        sc = jnp.where(mask, _dot_nt(q_mid[:, ds_], k_mid[:, ds_]), 0.0)
        o = _dot(sc.astype(bf16), vb[:, vs])
        o_int = None
        for (lo, hi, s_in, s_out) in segs:
            oi = _dot(q_dec[:, ds_], s_in[hd].astype(bf16))
            o_int = oi if o_int is None else jnp.where(rid >= lo, oi, o_int)
        outs.append(o + o_int)
        for (lo, hi, s_in, s_out) in segs:
            if ebl.shape[0] == 1:
                dec = jnp.transpose(jnp.broadcast_to(ebl[:, ds_], (8, GLA_DK)))[:, 0:1]
            else:
                r0 = max(hi - 8, 0)
                dec = jnp.transpose(ebl[r0:r0 + 8, ds_])[:, hi - 1 - r0:hi - r0]
            ke = k_end[:, ds_]
            if len(segs) > 1:
                ke = jnp.where((rid >= lo) & (rid < hi), ke, 0.0)
            s_out[hd] = s_in[hd] * dec + _dot_tn(ke.astype(bf16), vb[:, vs])
    return jnp.concatenate(outs, axis=1)


def _gla_prompt_kernel(qkv_ref, la_ref, o_ref, sn_ref, s_scr, *, nl):
    tm = qkv_ref.shape[0]
    C = GLA_C
    l = pl.program_id(1)

    @pl.when(l == 0)
    def _():
        s_scr[...] = jnp.zeros_like(s_scr)

    tri = _iota2((C, C), 0) >= _iota2((C, C), 1)
    trif = tri.astype(f32)

    def body(c, carry):
        rs = pl.ds(pl.multiple_of(c * C, C), C)
        b = _dot_hi(trif, la_ref[rs, :])
        bl = b[C - 1:C, :]
        anc = b[C // 2 - 1:C // 2, :]
        q = qkv_ref[rs, 0:GLA_QK]
        k = qkv_ref[rs, GLA_QK:2 * GLA_QK]
        v = qkv_ref[rs, 2 * GLA_QK:2 * GLA_QK + GLA_V]
        o_ref[rs, :] = _gla_chunk(q, k, v, b, bl, anc, tri, [(0, C, s_scr, s_scr)], C)
        return carry

    lax.fori_loop(0, tm // C, body, 0)

    @pl.when(l == nl - 1)
    def _():
        sn_ref[0] = s_scr[...]


def _gla_prompt(qkvr, la, nb, tm=256):
    T = qkvr.shape[0]
    nl = T // nb // tm
    W = 2 * GLA_QK + GLA_V
    return pl.pallas_call(
        functools.partial(_gla_prompt_kernel, nl=nl),
        out_shape=(jax.ShapeDtypeStruct((T, GLA_V), f32),
                   jax.ShapeDtypeStruct((nb, GLA_H, GLA_DK, GLA_DV), f32)),
        grid=(nb, nl),
        in_specs=[pl.BlockSpec((tm, W), lambda b, l: (b * nl + l, 0)),
                  pl.BlockSpec((tm, GLA_QK), lambda b, l: (b * nl + l, 0))],
        out_specs=(pl.BlockSpec((tm, GLA_V), lambda b, l: (b * nl + l, 0)),
                   pl.BlockSpec((1, GLA_H, GLA_DK, GLA_DV), lambda b, l: (b, 0, 0, 0))),
        scratch_shapes=[pltpu.VMEM((GLA_H, GLA_DK, GLA_DV), f32)],
        compiler_params=_params("arbitrary", "arbitrary"),
        name="gla_prompt",
    )(qkvr, la)


_GLA_SB = 16


def _gla_sample_kernel(qkv_ref, la_ref, s0_ref, o_ref, sn_ref, b_scr, bl_scr):
    rows = qkv_ref.shape[0]
    r = _iota2((rows, rows), 0)
    c = _iota2((rows, rows), 1)
    same = (r >> 2) == (c >> 2)
    la = la_ref[...]
    b_scr[...] = _dot_hi((same & (r >= c)).astype(f32), la)
    bl_scr[...] = _dot_hi(same.astype(f32), la)
    r8 = _iota2((8, 8), 0)
    c8 = _iota2((8, 8), 1)
    mask8 = ((r8 >> 2) == (c8 >> 2)) & (r8 >= c8)

    def body(p, carry):
        rs = pl.ds(pl.multiple_of(p * 8, 8), 8)
        q = qkv_ref[rs, 0:GLA_QK]
        k = qkv_ref[rs, GLA_QK:2 * GLA_QK]
        v = qkv_ref[rs, 2 * GLA_QK:2 * GLA_QK + GLA_V]
        segs = [(0, 4, s0_ref.at[2 * p], sn_ref.at[2 * p]),
                (4, 8, s0_ref.at[2 * p + 1], sn_ref.at[2 * p + 1])]
        o_ref[rs, :] = _gla_chunk(q, k, v, b_scr[rs, :], bl_scr[rs, :], 0.0, mask8, segs, 8)
        return carry

    lax.fori_loop(0, rows // 8, body, 0)


def _gla_sample(qkvr, la, s0):
    T = qkvr.shape[0]
    nb = s0.shape[0]
    rb = _GLA_SB * DEC_SEQ
    W = 2 * GLA_QK + GLA_V
    st = pl.BlockSpec((_GLA_SB, GLA_H, GLA_DK, GLA_DV), lambda i: (i, 0, 0, 0))
    return pl.pallas_call(
        _gla_sample_kernel,
        out_shape=(jax.ShapeDtypeStruct((T, GLA_V), f32),
                   jax.ShapeDtypeStruct((nb, GLA_H, GLA_DK, GLA_DV), f32)),
        grid=(T // rb,),
        in_specs=[pl.BlockSpec((rb, W), lambda i: (i, 0)), pl.BlockSpec((rb, GLA_QK), lambda i: (i, 0)), st],
        out_specs=(pl.BlockSpec((rb, GLA_V), lambda i: (i, 0)), st),
        scratch_shapes=[pltpu.VMEM((rb, GLA_QK), f32), pltpu.VMEM((rb, GLA_QK), f32)],
        compiler_params=_params("arbitrary"),
        name="gla_sample",
    )(qkvr, la, s0)


def _gla_post_kernel(o_ref, r_ref, x_ref, gn_ref, wout_ref, out_ref):
    o = o_ref[...]
    parts = []
    for hd in range(GLA_H):
        vs = slice(hd * GLA_DV, (hd + 1) * GLA_DV)
        parts.append(_rms(o[:, vs], gn_ref[:, vs]))
    on = jnp.concatenate(parts, axis=1)
    gated = (on * _silu(r_ref[...])).astype(bf16)
    out_ref[...] = x_ref[...] + _dot(gated, wout_ref[...])


def _gla_post(o, qkvr, x, gn, wout, tm=512):
    T = x.shape[0]
    row = pl.BlockSpec((tm, D), lambda i: (i, 0))
    return pl.pallas_call(
        _gla_post_kernel,
        out_shape=jax.ShapeDtypeStruct((T, D), f32),
        grid=(T // tm,),
        in_specs=[row, pl.BlockSpec((tm, GLA_V), lambda i: (i, 2)), row, _cspec((1, GLA_V)),
                  _cspec((GLA_V, D))],
        out_specs=row,
        compiler_params=_params("arbitrary"),
        name="gla_post",
    )(o, qkvr, x, gn.reshape(1, GLA_V), wout)


def _ssd_pre_kernel(x_ref, g_ref, wz_ref, wx_ref, wdt_ref, z_ref, xbc_ref, dt_ref):
    h = _rms(x_ref[...], g_ref[...]).astype(bf16)
    z_ref[...] = _dot(h, wz_ref[...])
    xbc_ref[...] = _dot(h, wx_ref[...])
    dt_ref[...] = _dot(h, wdt_ref[...])


def _ssd_pre(x, g, wz, wx, wdt, tm=512):
    T = x.shape[0]
    return pl.pallas_call(
        _ssd_pre_kernel,
        out_shape=(jax.ShapeDtypeStruct((T, SSM_DI), f32), jax.ShapeDtypeStruct((T, SSM_CONV_DIM), f32),
                   jax.ShapeDtypeStruct((T, SSM_H), f32)),
        grid=(T // tm,),
        in_specs=[pl.BlockSpec((tm, D), lambda i: (i, 0)), _cspec((1, D)), _cspec((D, SSM_DI)),
                  _cspec((D, SSM_CONV_DIM)), _cspec((D, SSM_H))],
        out_specs=(pl.BlockSpec((tm, SSM_DI), lambda i: (i, 0)),
                   pl.BlockSpec((tm, SSM_CONV_DIM), lambda i: (i, 0)),
                   pl.BlockSpec((tm, SSM_H), lambda i: (i, 0))),
        compiler_params=_params("arbitrary"),
        name="ssd_pre",
    )(x, g.reshape(1, D), wz, wx, wdt)


def _ssd_chunk(act_ref, rs, dt, a_row, dx_row, y_ref, mask, segs, C):
    maskf = mask.astype(f32)
    eye = (_iota2((C, C), 0) == _iota2((C, C), 1)).astype(f32)
    expand = ((_iota2((SSM_H, SSM_DI), 1) >> 6) == _iota2((SSM_H, SSM_DI), 0)).astype(f32)
    cum = _dot_hi(maskf, dt * a_row)
    cum_t = _dot_hi_tn(cum, eye)
    dt_t = _dot_hi_tn(dt, eye)
    cumx = _dot_hi(cum, expand)
    dtx = _dot_hi(dt, expand)
    rid = _iota2((C, 1), 0)
    lane_lo = _iota2((C, 2 * SSM_P), 1) < SSM_P
    for g in range(SSM_G):
        gs = slice(g * GD, (g + 1) * GD)
        bm = act_ref[rs, SSM_DI + g * SSM_N:SSM_DI + (g + 1) * SSM_N].astype(bf16)
        cm = act_ref[rs, SSM_DI + SSM_G * SSM_N + g * SSM_N:SSM_DI + SSM_G * SSM_N + (g + 1) * SSM_N].astype(bf16)
        cb = _dot_nt(cm, bm)
        xs = act_ref[rs, gs]
        cumg = cumx[:, gs]
        y_int = None
        for (lo, hi, s_in, s_out) in segs:
            yi = _dot_nt(cm, s_in[gs, :].astype(bf16))
            y_int = yi if y_int is None else jnp.where(rid >= lo, yi, y_int)
        yg = y_int * jnp.exp(cumg) + xs * dx_row[:, gs]
        for pr in range(SSM_HPG // 2):
            h0 = g * SSM_HPG + 2 * pr
            ps = slice(2 * pr * SSM_P, (2 * pr + 2) * SSM_P)
            xp = xs[:, ps].astype(bf16)
            acc = yg[:, ps]
            for hh, keep in ((h0, lane_lo), (h0 + 1, ~lane_lo)):
                seg = cum[:, hh:hh + 1] - cum_t[hh:hh + 1, :]
                mix = cb * jnp.exp(jnp.where(mask, seg, -jnp.inf)) * dt_t[hh:hh + 1, :]
                acc = acc + _dot(mix.astype(bf16), jnp.where(keep, xp, jnp.zeros_like(xp)))
            y_ref[rs, g * GD + 2 * pr * SSM_P:g * GD + (2 * pr + 2) * SSM_P] = acc
        for (lo, hi, s_in, s_out) in segs:
            wend = jnp.exp(cumg[hi - 1:hi, :] - cumg) * dtx[:, gs]
            xw = xs * wend
            if len(segs) > 1:
                xw = jnp.where((rid >= lo) & (rid < hi), xw, 0.0)
            upd = _dot_tn(xw.astype(bf16), bm)
            for r in range(SSM_HPG):
                hh = g * SSM_HPG + r
                hs = slice(hh * SSM_P, (hh + 1) * SSM_P)
                dec = jnp.exp(cum_t[hh:hh + 1, hi - 1:hi])
                s_out[hs, :] = s_in[hs, :] * dec + upd[r * SSM_P:(r + 1) * SSM_P, :]


def _conv_silu(terms, cw_ref, cb_ref):
    acc = terms[0] * cw_ref[0:1, :]
    for j in range(1, 4):
        acc = acc + terms[j] * cw_ref[j:j + 1, :]
    return _silu(cb_ref[...] + acc)


def _ssd_prompt_kernel(xbc_ref, dtr_ref, cw_ref, cb_ref, dtb_ref, alog_ref, dx_ref,
                       y_ref, sn_ref, cn_ref, ext_scr, act_scr, dt_scr, s_scr, *, nl):
    tm = xbc_ref.shape[0]
    C = SSM_C
    l = pl.program_id(1)

    @pl.when(l == 0)
    def _():
        ext_scr[0:8, :] = jnp.zeros((8, SSM_CONV_DIM), f32)
        s_scr[...] = jnp.zeros_like(s_scr)

    ext_scr[8:8 + tm, :] = xbc_ref[...]
    act_scr[...] = _conv_silu([ext_scr[5 + j:5 + j + tm, :] for j in range(4)], cw_ref, cb_ref)
    dt_scr[...] = jax.nn.softplus(dtr_ref[...] + dtb_ref[...])
    a_row = -jnp.exp(alog_ref[...])
    dx_row = dx_ref[...]
    tri = _iota2((C, C), 0) >= _iota2((C, C), 1)

    def body(c, carry):
        rs = pl.ds(pl.multiple_of(c * C, C), C)
        _ssd_chunk(act_scr, rs, dt_scr[rs, :], a_row, dx_row, y_ref, tri, [(0, C, s_scr, s_scr)], C)
        return carry

    lax.fori_loop(0, tm // C, body, 0)

    @pl.when(l == nl - 1)
    def _():
        sn_ref[0] = s_scr[...]
        cn_ref[0] = ext_scr[8 + tm - 3:8 + tm, :]

    ext_scr[0:8, :] = ext_scr[tm:tm + 8, :]


def _ssd_prompt(xbc, dtr, cw, cb, dtb, alog, dx, nb, tm=256):
    T = xbc.shape[0]
    nl = T // nb // tm
    rowmap = lambda b, l: (b * nl + l, 0)
    return pl.pallas_call(
        functools.partial(_ssd_prompt_kernel, nl=nl),
        out_shape=(jax.ShapeDtypeStruct((T, SSM_DI), f32),
                   jax.ShapeDtypeStruct((nb, SSM_DI, SSM_N), f32),
                   jax.ShapeDtypeStruct((nb, 3, SSM_CONV_DIM), f32)),
        grid=(nb, nl),
        in_specs=[pl.BlockSpec((tm, SSM_CONV_DIM), rowmap), pl.BlockSpec((tm, SSM_H), rowmap),
                  _cspec((4, SSM_CONV_DIM)), _cspec((1, SSM_CONV_DIM)), _cspec((1, SSM_H)),
                  _cspec((1, SSM_H)), _cspec((1, SSM_DI))],
        out_specs=(pl.BlockSpec((tm, SSM_DI), rowmap),
                   pl.BlockSpec((1, SSM_DI, SSM_N), lambda b, l: (b, 0, 0)),
                   pl.BlockSpec((1, 3, SSM_CONV_DIM), lambda b, l: (b, 0, 0))),
        scratch_shapes=[pltpu.VMEM((tm + 8, SSM_CONV_DIM), f32), pltpu.VMEM((tm, SSM_CONV_DIM), f32),
                        pltpu.VMEM((tm, SSM_H), f32), pltpu.VMEM((SSM_DI, SSM_N), f32)],
        compiler_params=_params("arbitrary", "arbitrary"),
        name="ssd_prompt",
    )(xbc, dtr, cw, cb, dtb, alog, dx)


def _ssd_sample_kernel(xbc_ref, dtr_ref, cs_ref, s0_ref, cw_ref, cb_ref, dtb_ref, alog_ref, dx_ref,
                       sel_ref, y_ref, sn_ref, act_scr, dt_scr):
    rows = xbc_ref.shape[0]
    xb = xbc_ref[...]
    t = _iota2((rows, 1), 0) & (DEC_SEQ - 1)
    parts = _split3(cs_ref[...])
    terms = []
    for j in range(4):
        sh = 3 - j
        term = xb if sh == 0 else jnp.where(t >= sh, pltpu.roll(xb, sh, axis=0), 0.0)
        if j < 3:
            term = term + _sel_dot(sel_ref[j], parts)
        terms.append(term)
    act_scr[...] = _conv_silu(terms, cw_ref, cb_ref)
    dt_scr[...] = jax.nn.softplus(dtr_ref[...] + dtb_ref[...])
    a_row = -jnp.exp(alog_ref[...])
    dx_row = dx_ref[...]
    r8 = _iota2((8, 8), 0)
    c8 = _iota2((8, 8), 1)
    mask8 = ((r8 >> 2) == (c8 >> 2)) & (r8 >= c8)

    def body(p, carry):
        rs = pl.ds(pl.multiple_of(p * 8, 8), 8)
        segs = [(0, 4, s0_ref.at[2 * p], sn_ref.at[2 * p]),
                (4, 8, s0_ref.at[2 * p + 1], sn_ref.at[2 * p + 1])]
        _ssd_chunk(act_scr, rs, dt_scr[rs, :], a_row, dx_row, y_ref, mask8, segs, 8)
        return carry

    lax.fori_loop(0, rows // 8, body, 0)


def _ssd_sample(xbc, dtr, cs2d, s0, cw, cb, dtb, alog, dx, sel):
    T = xbc.shape[0]
    nb = s0.shape[0]
    rb = _SB * DEC_SEQ
    cbk = _SB * 3
    st = pl.BlockSpec((_SB, SSM_DI, SSM_N), lambda i: (i, 0, 0))
    return pl.pallas_call(
        _ssd_sample_kernel,
        out_shape=(jax.ShapeDtypeStruct((T, SSM_DI), f32), jax.ShapeDtypeStruct((nb, SSM_DI, SSM_N), f32)),
        grid=(T // rb,),
        in_specs=[pl.BlockSpec((rb, SSM_CONV_DIM), lambda i: (i, 0)), pl.BlockSpec((rb, SSM_H), lambda i: (i, 0)),
                  pl.BlockSpec((cbk, SSM_CONV_DIM), lambda i: (i, 0)), st,
                  _cspec((4, SSM_CONV_DIM)), _cspec((1, SSM_CONV_DIM)), _cspec((1, SSM_H)),
                  _cspec((1, SSM_H)), _cspec((1, SSM_DI)), _cspec((3, rb, cbk))],
        out_specs=(pl.BlockSpec((rb, SSM_DI), lambda i: (i, 0)), st),
        scratch_shapes=[pltpu.VMEM((rb, SSM_CONV_DIM), f32), pltpu.VMEM((rb, SSM_H), f32)],
        compiler_params=_params("arbitrary"),
        name="ssd_sample",
    )(xbc, dtr, cs2d, s0, cw, cb, dtb, alog, dx, sel)


def _ssd_post_kernel(y_ref, z_ref, x_ref, gn_ref, wout_ref, out_ref):
    y = y_ref[...] * _silu(z_ref[...])
    parts = []
    for g in range(SSM_G):
        gs = slice(g * GD, (g + 1) * GD)
        parts.append(_rms(y[:, gs], gn_ref[:, gs]))
    yn = jnp.concatenate(parts, axis=1).astype(bf16)
    out_ref[...] = x_ref[...] + _dot(yn, wout_ref[...])


def _ssd_post(y, z, x, gn, wout, tm=512):
    T = x.shape[0]
    row = pl.BlockSpec((tm, D), lambda i: (i, 0))
    wide = pl.BlockSpec((tm, SSM_DI), lambda i: (i, 0))
    return pl.pallas_call(
        _ssd_post_kernel,
        out_shape=jax.ShapeDtypeStruct((T, D), f32),
        grid=(T // tm,),
        in_specs=[wide, wide, row, _cspec((1, SSM_DI)), _cspec((SSM_DI, D))],
        out_specs=row,
        compiler_params=_params("arbitrary"),
        name="ssd_post",
    )(y, z, x, gn.reshape(1, SSM_DI), wout)


def _pool_selectors():
    sel = np.zeros((4, _SB * DEC_SEQ, _SB * POOL_HIST), np.float32)
    for gi, w in enumerate(POOL_WINDOWS):
        for s in range(_SB):
            for t in range(DEC_SEQ):
                for i in range(POOL_HIST):
                    if i >= POOL_HIST + 1 + t - w:
                        sel[gi, s * DEC_SEQ + t, s * POOL_HIST + i] = 1.0
    return sel


def _conv_selectors():
    sel = np.zeros((3, _SB * DEC_SEQ, _SB * 3), np.float32)
    for j in range(3):
        for s in range(_SB):
            for t in range(DEC_SEQ):
                if t + j < 3:
                    sel[j, s * DEC_SEQ + t, s * 3 + t + j] = 1.0
    return sel


def kernel(x_prompt, x_sample, state_pool_l1, state_gla_l2, state_ssm_l3, state_conv_l3, p_prompt, p_sample, norm_ffn1, ffn1_gate, ffn1_up, ffn1_down, norm_mix, norm_ffn2, ffn2_gate, ffn2_up, ffn2_down, norm_ple, ple_gate, ple_proj, norm_final, gm_w_in, gm_ln, gm_w_s, gm_b_s, gm_w_out, pool_w, pool_scale, gla_w_in, gla_w_a1, gla_w_a2, gla_b_a, gla_norm, gla_w_out, ssm_w_in, ssm_conv_w, ssm_conv_b, ssm_dt_bias, ssm_a_log, ssm_d, ssm_norm, ssm_w_out):
    nbp, seq, _ = x_prompt.shape
    nbs, dseq, _ = x_sample.shape
    assert dseq == DEC_SEQ and seq % 512 == 0 and nbs % _GLA_SB == 0
    cast = lambda w: w.astype(bf16)

    w1g, w1u, w1d = cast(ffn1_gate), cast(ffn1_up), cast(ffn1_down)
    w2g, w2u, w2d = cast(ffn2_gate), cast(ffn2_up), cast(ffn2_down)
    wpg, wpp = cast(ple_gate), cast(ple_proj)
    gm_in, gm_out = cast(gm_w_in), cast(gm_w_out)
    pw = cast(pool_w)
    gla_in, gla_a1, gla_a2, gla_out = cast(gla_w_in), cast(gla_w_a1), cast(gla_w_a2), cast(gla_w_out)
    ssm_wz = cast(ssm_w_in[:, :SSM_DI])
    ssm_wx = cast(ssm_w_in[:, SSM_DI:SSM_DI + SSM_CONV_DIM])
    ssm_wdt = cast(ssm_w_in[:, SSM_DI + SSM_CONV_DIM:])
    ssm_out = cast(ssm_w_out)

    dx = jnp.repeat(ssm_d, SSM_P).reshape(1, SSM_DI)
    dtb = ssm_dt_bias.reshape(1, SSM_H)
    alog = ssm_a_log.reshape(1, SSM_H)
    cbias = ssm_conv_b.reshape(1, SSM_CONV_DIM)
    bst = jnp.transpose(gm_b_s)

    t8 = np.arange(8) % DEC_SEQ
    wj = []
    for j in range(DEC_SEQ):
        src = t8 - j
        vals = gm_w_s[:, t8, np.maximum(src, 0)]
        vals = jnp.where(jnp.asarray(src >= 0)[None, :], vals, 0.0)
        wj.append(jnp.repeat(jnp.transpose(vals), 128, axis=1))
    wj = jnp.stack(wj)
    bt = jnp.repeat(jnp.transpose(gm_b_s[:, t8]), 128, axis=1)

    pool_sel = jnp.asarray(_pool_selectors(), bf16)
    conv_sel = jnp.asarray(_conv_selectors(), bf16)

    xp = x_prompt.reshape(nbp * seq, D)
    xs = x_sample.reshape(nbs * dseq, D)
    pp = p_prompt.reshape(DEPTH, nbp * seq, D_PLE)
    ps = p_sample.reshape(DEPTH, nbs * dseq, D_PLE)
    hist2d = state_pool_l1.reshape(nbs * POOL_HIST, D)
    cs2d = state_conv_l3.reshape(nbs * 3, SSM_CONV_DIM)
    ssm_s0 = state_ssm_l3.reshape(nbs, SSM_DI, SSM_N)

    outs = {}
    for i in range(DEPTH):
        xp = _ffn(xp, norm_ffn1[i], w1g[i], w1u[i], w1d[i])
        xs = _ffn(xs, norm_ffn1[i], w1g[i], w1u[i], w1d[i])
        gmix = norm_mix[i]
        if i == 0:
            xp = _gmlp_prompt(xp, gmix, gm_in, gm_ln, gm_w_s, bst, gm_out)
            xs, chunk_v = _gmlp_sample(xs, gmix, gm_in, gm_ln, wj, bt, gm_out)
            outs["chunk_v"] = chunk_v.reshape(nbs, dseq, D)
        elif i == 1:
            xp, outs["pool_p"] = _pool_prompt(xp, gmix, pw, pool_scale, nbp)
            xs, hs = _pool_sample(xs, hist2d, gmix, pw, pool_scale, pool_sel, PAST_LEN)
            outs["pool_s"] = jnp.concatenate([state_pool_l1[:, dseq:], hs.reshape(nbs, dseq, D)], axis=1)
        elif i == 2:
            qkvr, la = _gla_pre(xp, gmix, gla_in, gla_a1, gla_a2, gla_b_a)
            o, outs["gla_p"] = _gla_prompt(qkvr, la, nbp)
            xp = _gla_post(o, qkvr, xp, gla_norm, gla_out)
            qkvr, la = _gla_pre(xs, gmix, gla_in, gla_a1, gla_a2, gla_b_a)
            o, outs["gla_s"] = _gla_sample(qkvr, la, state_gla_l2)
            xs = _gla_post(o, qkvr, xs, gla_norm, gla_out)
        else:
            z, xbc, dtr = _ssd_pre(xp, gmix, ssm_wz, ssm_wx, ssm_wdt)
            y, sn, outs["conv_p"] = _ssd_prompt(xbc, dtr, ssm_conv_w, cbias, dtb, alog, dx, nbp)
            outs["ssm_p"] = sn.reshape(nbp, SSM_H, SSM_P, SSM_N)
            xp = _ssd_post(y, z, xp, ssm_norm, ssm_out)
            z, xbc, dtr = _ssd_pre(xs, gmix, ssm_wz, ssm_wx, ssm_wdt)
            y, sn = _ssd_sample(xbc, dtr, cs2d, ssm_s0, ssm_conv_w, cbias, dtb, alog, dx, conv_sel)
            outs["ssm_s"] = sn.reshape(nbs, SSM_H, SSM_P, SSM_N)
            outs["conv_s"] = xbc.reshape(nbs, dseq, SSM_CONV_DIM)[:, dseq - 3:]
            xs = _ssd_post(y, z, xs, ssm_norm, ssm_out)
        fin = norm_final if i == DEPTH - 1 else None
        ple = lambda p2d: (p2d, norm_ple[i], wpg[i], wpp[i])
        xp = _ffn(xp, norm_ffn2[i], w2g[i], w2u[i], w2d[i], ple=ple(pp[i]), final_g=fin)
        xs = _ffn(xs, norm_ffn2[i], w2g[i], w2u[i], w2d[i], ple=ple(ps[i]), final_g=fin)

    return (xp.reshape(nbp, seq, D), xs.reshape(nbs, dseq, D), outs["chunk_v"], outs["pool_p"], outs["pool_s"],
            outs["gla_p"], outs["gla_s"], outs["ssm_p"], outs["ssm_s"], outs["conv_p"], outs["conv_s"])
```

```python
import functools

import jax
import jax.numpy as jnp
import numpy as np
from jax import lax
from jax.experimental import pallas as pl
from jax.experimental.pallas import tpu as pltpu

f32 = jnp.float32
bf16 = jnp.bfloat16

D = 1024
F = 2816
FC = 256
NF = F // FC
D_PLE = 256
EPS = 1e-6
DEPTH = 4

GM_CHUNK = 128
GM_GROUPS = 8
POOL_WINDOWS = (2, 4, 8, 16)
POOL_GD = D // 4
POOL_HIST = 15

GLA_H = 4
GLA_DK = 128
GLA_DV = 256
GLA_QK = GLA_H * GLA_DK
GLA_V = GLA_H * GLA_DV
GLA_NORMALIZER = 16.0
GLA_C = 64

SSM_DI = 2048
SSM_P = 64
SSM_H = 32
SSM_G = 4
SSM_HPG = 8
SSM_N = 128
SSM_CONV_DIM = SSM_DI + 2 * SSM_G * SSM_N
SSM_C = 128
GD = SSM_HPG * SSM_P

DEC_SEQ = 4
PAST_LEN = 16384

VMEM_LIMIT = 56 * 1024 * 1024


def _dot(a, b):
    return jnp.dot(a, b, preferred_element_type=f32)


def _dot_nt(a, b):
    return lax.dot_general(a, b, (((1,), (1,)), ((), ())), preferred_element_type=f32)


def _dot_tn(a, b):
    return lax.dot_general(a, b, (((0,), (0,)), ((), ())), preferred_element_type=f32)


def _dot_hi(a, b):
    return jnp.dot(a, b, precision=lax.Precision.HIGHEST, preferred_element_type=f32)


def _dot_hi_tn(a, b):
    return lax.dot_general(a, b, (((0,), (0,)), ((), ())), precision=lax.Precision.HIGHEST,
                           preferred_element_type=f32)


def _split3(x):
    hi = x.astype(bf16)
    r1 = x - hi.astype(f32)
    mid = r1.astype(bf16)
    lo = (r1 - mid.astype(f32)).astype(bf16)
    return hi, mid, lo


def _sel_dot(sel, parts):
    hi, mid, lo = parts
    return _dot(sel, hi) + _dot(sel, mid) + _dot(sel, lo)


def _rms(x, g):
    return x * lax.rsqrt(jnp.mean(x * x, axis=-1, keepdims=True) + EPS) * g


def _sigmoid(x):
    return 0.5 + 0.5 * jnp.tanh(0.5 * x)


def _silu(x):
    hx = 0.5 * x
    return hx + hx * jnp.tanh(hx)


def _cspec(shape):
    nd = len(shape)
    return pl.BlockSpec(shape, lambda *_: (0,) * nd, pipeline_mode=pl.Buffered(1))


def _params(*sem):
    return pltpu.CompilerParams(dimension_semantics=tuple(sem), vmem_limit_bytes=VMEM_LIMIT)


def _iota2(shape, dim):
    return lax.broadcasted_iota(jnp.int32, shape, dim)


def _ffn_kernel(*refs, ple, final, n_prompt):
    xp_ref, xs_ref, g_ref, wg_ref, wu_ref, wd_ref = refs[:6]
    k = 6
    if ple:
        pp_ref, ps_ref, gp_ref, wpg_ref, wpp_ref = refs[k:k + 5]
        k += 5
    if final:
        gf_ref = refs[k]
        k += 1
    op_ref, os_ref, h_scr, acc_scr = refs[k:k + 4]

    is_prompt = pl.program_id(0) < n_prompt
    x = jnp.where(is_prompt, xp_ref[...], xs_ref[...])
    h_scr[...] = _rms(x, g_ref[...]).astype(bf16)
    for j in range(NF):
        sl = slice(j * FC, (j + 1) * FC)
        hb = h_scr[...]
        a = _silu(_dot(hb, wg_ref[:, sl])) * _dot(hb, wu_ref[:, sl])
        contrib = _dot(a.astype(bf16), wd_ref[sl, :])
        if j == 0:
            acc_scr[...] = contrib
        else:
            acc_scr[...] += contrib
    y = x + 0.5 * acc_scr[...]
    if ple:
        h2 = _rms(y, gp_ref[...]).astype(bf16)
        gate = _sigmoid(_dot(h2, wpg_ref[...]))
        p = jnp.where(is_prompt, pp_ref[...], ps_ref[...]).astype(bf16)
        y = y + gate * _dot(p, wpp_ref[...])
    if final:
        y = _rms(y, gf_ref[...])

    @pl.when(is_prompt)
    def _():
        op_ref[...] = y

    @pl.when(jnp.logical_not(is_prompt))
    def _():
        os_ref[...] = y


def _lspec(shape, layer):
    nd = len(shape)
    return pl.BlockSpec((None,) + tuple(shape), lambda *_: (layer,) + (0,) * nd,
                        pipeline_mode=pl.Buffered(1))


def _ffn(xp, xs, layer, g, wg, wu, wd, ple=None, final_g=None):
    tm = xs.shape[0]
    n_prompt = xp.shape[0] // tm
    prow = lambda w: pl.BlockSpec((tm, w), lambda i: (jnp.minimum(i, n_prompt - 1), 0))
    srow = lambda w: pl.BlockSpec((tm, w), lambda i: (0, 0))
    in_specs = [prow(D), srow(D), _lspec((1, D), layer), _lspec((D, F), layer), _lspec((D, F), layer),
                _lspec((F, D), layer)]
    args = [xp, xs, g, wg, wu, wd]
    if ple is not None:
        pp, ps, gp, wpg, wpp = ple
        in_specs += [pl.BlockSpec((None, tm, D_PLE), lambda i: (layer, jnp.minimum(i, n_prompt - 1), 0)),
                     pl.BlockSpec((None, tm, D_PLE), lambda i: (layer, 0, 0)),
                     _lspec((1, D), layer), _lspec((D, D), layer), _lspec((D_PLE, D), layer)]
        args += [pp, ps, gp, wpg, wpp]
    if final_g is not None:
        in_specs.append(_cspec((1, D)))
        args.append(final_g.reshape(1, D))
    return pl.pallas_call(
        functools.partial(_ffn_kernel, ple=ple is not None, final=final_g is not None, n_prompt=n_prompt),
        out_shape=(jax.ShapeDtypeStruct(xp.shape, f32), jax.ShapeDtypeStruct(xs.shape, f32)),
        grid=(n_prompt + 1,),
        in_specs=in_specs,
        out_specs=(prow(D), srow(D)),
        scratch_shapes=[pltpu.VMEM((tm, D), bf16), pltpu.VMEM((tm, D), f32)],
        compiler_params=_params("arbitrary"),
        name="ffn",
    )(*args)


def _gmlp_pre(x, g, win_ref, ln_ref):
    h = _rms(x, g).astype(bf16)
    y = jax.nn.gelu(_dot(h, win_ref[...]), approximate=True)
    u = y[:, :D]
    v = y[:, D:]
    vc = v - jnp.mean(v, axis=-1, keepdims=True)
    v = vc * lax.rsqrt(jnp.mean(vc * vc, axis=-1, keepdims=True) + EPS) * ln_ref[...]
    return u, v


def _gmlp_prompt_kernel(x_ref, g_ref, win_ref, ln_ref, ws_ref, bst_ref, wout_ref, o_ref,
                        u_scr, vb_scr, t_scr):
    tm = x_ref.shape[0]
    x = x_ref[...]
    u, v = _gmlp_pre(x, g_ref[...], win_ref, ln_ref)
    u_scr[...] = u
    vb_scr[...] = v.astype(bf16)
    causal = _iota2((GM_CHUNK, GM_CHUNK), 0) >= _iota2((GM_CHUNK, GM_CHUNK), 1)
    for g in range(GM_GROUPS):
        wm = jnp.where(causal, ws_ref[g], 0.0).astype(bf16)
        bias = bst_ref[:, g:g + 1]
        cs = slice(g * 128, (g + 1) * 128)
        for ch in range(tm // GM_CHUNK):
            rs = slice(ch * GM_CHUNK, (ch + 1) * GM_CHUNK)
            sv = _dot(wm, vb_scr[rs, cs]) + bias
            t_scr[rs, cs] = (u_scr[rs, cs] * sv).astype(bf16)
    o_ref[...] = x + _dot(t_scr[...], wout_ref[...])


def _gmlp_prompt(x, g, win, ln, ws, bst, wout, tm=512):
    T = x.shape[0]
    row = pl.BlockSpec((tm, D), lambda i: (i, 0))
    return pl.pallas_call(
        _gmlp_prompt_kernel,
        out_shape=jax.ShapeDtypeStruct((T, D), f32),
        grid=(T // tm,),
        in_specs=[row, _cspec((1, D)), _cspec((D, 2 * D)), _cspec((1, D)),
                  _cspec((GM_GROUPS, GM_CHUNK, GM_CHUNK)), _cspec((GM_CHUNK, GM_GROUPS)),
                  _cspec((D, D))],
        out_specs=row,
        scratch_shapes=[pltpu.VMEM((tm, D), f32), pltpu.VMEM((tm, D), bf16), pltpu.VMEM((tm, D), bf16)],
        compiler_params=_params("arbitrary"),
        name="gmlp_prompt",
    )(x, g.reshape(1, D), win, ln.reshape(1, D), ws, bst, wout)


def _gmlp_sample_kernel(x_ref, g_ref, win_ref, ln_ref, wj_ref, bt_ref, wout_ref, o_ref, v_ref):
    rows = x_ref.shape[0]
    n = rows // 8
    x = x_ref[...]
    u, v = _gmlp_pre(x, g_ref[...], win_ref, ln_ref)
    v_ref[...] = v
    acc = v.reshape(n, 8, D) * wj_ref[0][None]
    for j in range(1, DEC_SEQ):
        acc = acc + pltpu.roll(v, j, axis=0).reshape(n, 8, D) * wj_ref[j][None]
    sv = (acc + bt_ref[...][None]).reshape(rows, D)
    o_ref[...] = x + _dot((u * sv).astype(bf16), wout_ref[...])


def _gmlp_sample(x, g, win, ln, wj, bt, wout):
    T = x.shape[0]
    full = pl.BlockSpec((T, D), lambda i: (0, 0))
    return pl.pallas_call(
        _gmlp_sample_kernel,
        out_shape=(jax.ShapeDtypeStruct((T, D), f32), jax.ShapeDtypeStruct((T, D), f32)),
        grid=(1,),
        in_specs=[full, _cspec((1, D)), _cspec((D, 2 * D)), _cspec((1, D)),
                  _cspec((DEC_SEQ, 8, D)), _cspec((8, D)), _cspec((D, D))],
        out_specs=(full, full),
        compiler_params=_params("arbitrary"),
        name="gmlp_sample",
    )(x, g.reshape(1, D), win, ln.reshape(1, D), wj, bt, wout)


def _pool_prompt_kernel(x_ref, g_ref, pw_ref, sc_ref, o_ref, pn_ref, ext_scr, *, nl):
    tm = x_ref.shape[0]
    l = pl.program_id(1)

    @pl.when(l == 0)
    def _():
        ext_scr[0:16, :] = jnp.zeros((16, D), f32)

    x = x_ref[...]
    ext_scr[16:16 + tm, :] = _rms(x, g_ref[...])
    pos = l * tm + _iota2((tm, 1), 0)
    for gi, w in enumerate(POOL_WINDOWS):
        cs = slice(gi * POOL_GD, (gi + 1) * POOL_GD)
        h = ext_scr[16:16 + tm, cs]
        s = h
        for j in range(1, w):
            s = s + ext_scr[16 - j:16 - j + tm, cs]
        cnt = jnp.minimum(pos + 1, w).astype(f32)
        diff = (s / cnt - h).astype(bf16)
        o_ref[:, cs] = x[:, cs] + _dot(diff, pw_ref[gi]) * sc_ref[:, cs]

    @pl.when(l == nl - 1)
    def _():
        pn_ref[0] = ext_scr[16 + tm - POOL_HIST:16 + tm, :]

    ext_scr[0:16, :] = ext_scr[tm:tm + 16, :]


def _pool_prompt(x, g, pw, sc, nb, tm=512):
    T = x.shape[0]
    nl = T // nb // tm
    row = pl.BlockSpec((tm, D), lambda b, l: (b * nl + l, 0))
    return pl.pallas_call(
        functools.partial(_pool_prompt_kernel, nl=nl),
        out_shape=(jax.ShapeDtypeStruct((T, D), f32), jax.ShapeDtypeStruct((nb, POOL_HIST, D), f32)),
        grid=(nb, nl),
        in_specs=[row, _cspec((1, D)), _cspec((4, POOL_GD, POOL_GD)), _cspec((1, D))],
        out_specs=(row, pl.BlockSpec((1, POOL_HIST, D), lambda b, l: (b, 0, 0))),
        scratch_shapes=[pltpu.VMEM((tm + 16, D), f32)],
        compiler_params=_params("arbitrary", "arbitrary"),
        name="pool_prompt",
    )(x, g.reshape(1, D), pw, sc.reshape(1, D))


_SB = 8


def _pool_sample_kernel(x_ref, hist_ref, g_ref, pw_ref, sc_ref, sel_ref, o_ref, h_ref, *, pos0):
    rows = x_ref.shape[0]
    x = x_ref[...]
    h = _rms(x, g_ref[...])
    h_ref[...] = h
    t = _iota2((rows, 1), 0) & (DEC_SEQ - 1)
    parts = _split3(hist_ref[...])
    for gi, w in enumerate(POOL_WINDOWS):
        cs = slice(gi * POOL_GD, (gi + 1) * POOL_GD)
        hc = h[:, cs]
        s = hc
        for j in range(1, min(w, DEC_SEQ)):
            s = s + jnp.where(t >= j, pltpu.roll(hc, j, axis=0), 0.0)
        s = s + _sel_dot(sel_ref[gi], tuple(p[:, cs] for p in parts))
        cnt = jnp.minimum(pos0 + t + 1, w).astype(f32)
        diff = (s / cnt - hc).astype(bf16)
        o_ref[:, cs] = x[:, cs] + _dot(diff, pw_ref[gi]) * sc_ref[:, cs]


def _pool_sample(x, hist2d, g, pw, sc, sel, pos0):
    T = x.shape[0]
    rb = _SB * DEC_SEQ
    hb = _SB * POOL_HIST
    row = pl.BlockSpec((rb, D), lambda i: (i, 0))
    return pl.pallas_call(
        functools.partial(_pool_sample_kernel, pos0=pos0),
        out_shape=(jax.ShapeDtypeStruct((T, D), f32), jax.ShapeDtypeStruct((T, D), f32)),
        grid=(T // rb,),
        in_specs=[row, pl.BlockSpec((hb, D), lambda i: (i, 0)), _cspec((1, D)),
                  _cspec((4, POOL_GD, POOL_GD)), _cspec((1, D)), _cspec((4, rb, hb))],
        out_specs=(row, row),
        compiler_params=_params("arbitrary"),
        name="pool_sample",
    )(x, hist2d, g.reshape(1, D), pw, sc.reshape(1, D), sel)


def _cumsum_rows(x):
    n = x.shape[0]
    rid = _iota2((n, 1), 0)
    s = 1
    while s < n:
        x = x + jnp.where(rid >= s, pltpu.roll(x, s, axis=0), 0.0)
        s *= 2
    return x


def _gla_pre_body(x, g_ref, win_ref, wa1_ref, wa2_ref, ba_ref, qkvr_ref, la_ref):
    h = _rms(x, g_ref[...]).astype(bf16)
    qkvr_ref[...] = _dot(h, win_ref[...])
    a1 = _dot(h, wa1_ref[...]).astype(bf16)
    z = _dot(a1, wa2_ref[...]) + ba_ref[...]
    la_ref[...] = jax.nn.log_sigmoid(z) / GLA_NORMALIZER


def _gla_post_body(o, r, x, gn_ref, wout_ref):
    parts = []
    for hd in range(GLA_H):
        vs = slice(hd * GLA_DV, (hd + 1) * GLA_DV)
        parts.append(_rms(o[:, vs], gn_ref[:, vs]))
    on = jnp.concatenate(parts, axis=1)
    gated = (on * _silu(r)).astype(bf16)
    return x + _dot(gated, wout_ref[...])


def _gla_pre_kernel(x_ref, g_ref, win_ref, wa1_ref, wa2_ref, ba_ref, qkvr_ref, la_ref):
    _gla_pre_body(x_ref[...], g_ref, win_ref, wa1_ref, wa2_ref, ba_ref, qkvr_ref, la_ref)


def _gla_pre(x, g, win, wa1, wa2, ba, tm=512):
    T = x.shape[0]
    W = 2 * GLA_QK + 2 * GLA_V
    return pl.pallas_call(
        _gla_pre_kernel,
        out_shape=(jax.ShapeDtypeStruct((T, W), f32), jax.ShapeDtypeStruct((T, GLA_QK), f32)),
        grid=(T // tm,),
        in_specs=[pl.BlockSpec((tm, D), lambda i: (i, 0)), _cspec((1, D)), _cspec((D, W)),
                  _cspec((D, 16)), _cspec((16, GLA_QK)), _cspec((1, GLA_QK))],
        out_specs=(pl.BlockSpec((tm, W), lambda i: (i, 0)), pl.BlockSpec((tm, GLA_QK), lambda i: (i, 0))),
        compiler_params=_params("arbitrary"),
        name="gla_pre",
    )(x, g.reshape(1, D), win, wa1, wa2, ba.reshape(1, GLA_QK))


def _gla_chunk(q, k, v, b, bl, anc, mask, segs, C):
    qs = q * (GLA_DK ** -0.5)
    q_dec = (qs * jnp.exp(b)).astype(bf16)
    k_end = k * jnp.exp(bl - b)
    q_mid = (qs * jnp.exp(b - anc)).astype(bf16)
    k_mid = (k * jnp.exp(anc - b)).astype(bf16)
    ebl = jnp.exp(bl)
    vb = v.astype(bf16)
    rid = _iota2((C, 1), 0)
    outs = []
    for hd in range(GLA_H):
        ds_ = slice(hd * GLA_DK, (hd + 1) * GLA_DK)
        vs = slice(hd * GLA_DV, (hd + 1) * GLA_DV)
        sc = jnp.where(mask, _dot_nt(q_mid[:, ds_], k_mid[:, ds_]), 0.0)
        o = _dot(sc.astype(bf16), vb[:, vs])
        o_int = None
        for (lo, hi, s_in, s_out) in segs:
            oi = _dot(q_dec[:, ds_], s_in[hd].astype(bf16))
            o_int = oi if o_int is None else jnp.where(rid >= lo, oi, o_int)
        outs.append(o + o_int)
        for (lo, hi, s_in, s_out) in segs:
            if ebl.shape[0] == 1:
                dec = jnp.transpose(jnp.broadcast_to(ebl[:, ds_], (8, GLA_DK)))[:, 0:1]
            else:
                r0 = max(hi - 8, 0)
                dec = jnp.transpose(ebl[r0:r0 + 8, ds_])[:, hi - 1 - r0:hi - r0]
            ke = k_end[:, ds_]
            if len(segs) > 1:
                ke = jnp.where((rid >= lo) & (rid < hi), ke, 0.0)
            s_out[hd] = s_in[hd] * dec + _dot_tn(ke.astype(bf16), vb[:, vs])
    return jnp.concatenate(outs, axis=1)


def _gla_prompt_kernel(x_ref, g_ref, win_ref, wa1_ref, wa2_ref, ba_ref, gn_ref, wout_ref,
                       out_ref, sn_ref, qkvr_scr, la_scr, o_scr, s_scr, *, nl):
    tm = x_ref.shape[0]
    C = GLA_C
    l = pl.program_id(1)

    @pl.when(l == 0)
    def _():
        s_scr[...] = jnp.zeros_like(s_scr)

    x = x_ref[...]
    _gla_pre_body(x, g_ref, win_ref, wa1_ref, wa2_ref, ba_ref, qkvr_scr, la_scr)
    tri = _iota2((C, C), 0) >= _iota2((C, C), 1)
    for c in range(tm // C):
        rs = slice(c * C, (c + 1) * C)
        b = _cumsum_rows(la_scr[rs, :])
        bl = b[C - 1:C, :]
        anc = b[C // 2 - 1:C // 2, :]
        q = qkvr_scr[rs, 0:GLA_QK]
        k = qkvr_scr[rs, GLA_QK:2 * GLA_QK]
        v = qkvr_scr[rs, 2 * GLA_QK:2 * GLA_QK + GLA_V]
        o_scr[rs, :] = _gla_chunk(q, k, v, b, bl, anc, tri, [(0, C, s_scr, s_scr)], C)
    out_ref[...] = _gla_post_body(o_scr[...], qkvr_scr[:, 2 * GLA_QK + GLA_V:], x, gn_ref, wout_ref)

    @pl.when(l == nl - 1)
    def _():
        sn_ref[0] = s_scr[...]


def _gla_prompt(x, g, win, wa1, wa2, ba, gn, wout, nb, tm=256):
    T = x.shape[0]
    nl = T // nb // tm
    W = 2 * GLA_QK + 2 * GLA_V
    row = pl.BlockSpec((tm, D), lambda b, l: (b * nl + l, 0))
    return pl.pallas_call(
        functools.partial(_gla_prompt_kernel, nl=nl),
        out_shape=(jax.ShapeDtypeStruct((T, D), f32),
                   jax.ShapeDtypeStruct((nb, GLA_H, GLA_DK, GLA_DV), f32)),
        grid=(nb, nl),
        in_specs=[row, _cspec((1, D)), _cspec((D, W)), _cspec((D, 16)), _cspec((16, GLA_QK)),
                  _cspec((1, GLA_QK)), _cspec((1, GLA_V)), _cspec((GLA_V, D))],
        out_specs=(row, pl.BlockSpec((1, GLA_H, GLA_DK, GLA_DV), lambda b, l: (b, 0, 0, 0))),
        scratch_shapes=[pltpu.VMEM((tm, W), f32), pltpu.VMEM((tm, GLA_QK), f32), pltpu.VMEM((tm, GLA_V), f32),
                        pltpu.VMEM((GLA_H, GLA_DK, GLA_DV), f32)],
        compiler_params=_params("arbitrary", "arbitrary"),
        name="gla_prompt",
    )(x, g.reshape(1, D), win, wa1, wa2, ba.reshape(1, GLA_QK), gn.reshape(1, GLA_V), wout)


_GLA_SB = 16


def _gla_sample_kernel(qkv_ref, la_ref, s0_ref, o_ref, sn_ref, b_scr, bl_scr):
    rows = qkv_ref.shape[0]
    r = _iota2((rows, rows), 0)
    c = _iota2((rows, rows), 1)
    same = (r >> 2) == (c >> 2)
    la = la_ref[...]
    b_scr[...] = _dot_hi((same & (r >= c)).astype(f32), la)
    bl_scr[...] = _dot_hi(same.astype(f32), la)
    r8 = _iota2((8, 8), 0)
    c8 = _iota2((8, 8), 1)
    mask8 = ((r8 >> 2) == (c8 >> 2)) & (r8 >= c8)

    def body(p, carry):
        rs = pl.ds(pl.multiple_of(p * 8, 8), 8)
        q = qkv_ref[rs, 0:GLA_QK]
        k = qkv_ref[rs, GLA_QK:2 * GLA_QK]
        v = qkv_ref[rs, 2 * GLA_QK:2 * GLA_QK + GLA_V]
        segs = [(0, 4, s0_ref.at[2 * p], sn_ref.at[2 * p]),
                (4, 8, s0_ref.at[2 * p + 1], sn_ref.at[2 * p + 1])]
        o_ref[rs, :] = _gla_chunk(q, k, v, b_scr[rs, :], bl_scr[rs, :], 0.0, mask8, segs, 8)
        return carry

    lax.fori_loop(0, rows // 8, body, 0)


def _gla_sample(qkvr, la, s0):
    T = qkvr.shape[0]
    nb = s0.shape[0]
    rb = _GLA_SB * DEC_SEQ
    W = 2 * GLA_QK + GLA_V
    st = pl.BlockSpec((_GLA_SB, GLA_H, GLA_DK, GLA_DV), lambda i: (i, 0, 0, 0))
    return pl.pallas_call(
        _gla_sample_kernel,
        out_shape=(jax.ShapeDtypeStruct((T, GLA_V), f32),
                   jax.ShapeDtypeStruct((nb, GLA_H, GLA_DK, GLA_DV), f32)),
        grid=(T // rb,),
        in_specs=[pl.BlockSpec((rb, W), lambda i: (i, 0)), pl.BlockSpec((rb, GLA_QK), lambda i: (i, 0)), st],
        out_specs=(pl.BlockSpec((rb, GLA_V), lambda i: (i, 0)), st),
        scratch_shapes=[pltpu.VMEM((rb, GLA_QK), f32), pltpu.VMEM((rb, GLA_QK), f32)],
        compiler_params=_params("arbitrary"),
        name="gla_sample",
    )(qkvr, la, s0)


def _gla_post_kernel(o_ref, r_ref, x_ref, gn_ref, wout_ref, out_ref):
    out_ref[...] = _gla_post_body(o_ref[...], r_ref[...], x_ref[...], gn_ref, wout_ref)


def _gla_post(o, qkvr, x, gn, wout, tm=512):
    T = x.shape[0]
    row = pl.BlockSpec((tm, D), lambda i: (i, 0))
    return pl.pallas_call(
        _gla_post_kernel,
        out_shape=jax.ShapeDtypeStruct((T, D), f32),
        grid=(T // tm,),
        in_specs=[row, pl.BlockSpec((tm, GLA_V), lambda i: (i, 2)), row, _cspec((1, GLA_V)),
                  _cspec((GLA_V, D))],
        out_specs=row,
        compiler_params=_params("arbitrary"),
        name="gla_post",
    )(o, qkvr, x, gn.reshape(1, GLA_V), wout)


def _ssd_pre_kernel(x_ref, g_ref, win_ref, z_ref, xbc_ref, dt_ref):
    h = _rms(x_ref[...], g_ref[...]).astype(bf16)
    z_ref[...] = _dot(h, win_ref[:, 0:SSM_DI])
    xbc_ref[...] = _dot(h, win_ref[:, SSM_DI:SSM_DI + SSM_CONV_DIM])
    dt_ref[...] = _dot(h, win_ref[:, SSM_DI + SSM_CONV_DIM:])


def _ssd_pre(x, g, win, tm=512):
    T = x.shape[0]
    return pl.pallas_call(
        _ssd_pre_kernel,
        out_shape=(jax.ShapeDtypeStruct((T, SSM_DI), f32), jax.ShapeDtypeStruct((T, SSM_CONV_DIM), f32),
                   jax.ShapeDtypeStruct((T, SSM_H), f32)),
        grid=(T // tm,),
        in_specs=[pl.BlockSpec((tm, D), lambda i: (i, 0)), _cspec((1, D)), _cspec((D, win.shape[1]))],
        out_specs=(pl.BlockSpec((tm, SSM_DI), lambda i: (i, 0)),
                   pl.BlockSpec((tm, SSM_CONV_DIM), lambda i: (i, 0)),
                   pl.BlockSpec((tm, SSM_H), lambda i: (i, 0))),
        compiler_params=_params("arbitrary"),
        name="ssd_pre",
    )(x, g.reshape(1, D), win)


def _ssd_chunk(act_ref, rs, dt, a_row, dx_row, y_ref, mask, segs, C):
    maskf = mask.astype(f32)
    eye = (_iota2((C, C), 0) == _iota2((C, C), 1)).astype(f32)
    expand = ((_iota2((SSM_H, SSM_DI), 1) >> 6) == _iota2((SSM_H, SSM_DI), 0)).astype(f32)
    cum = _dot_hi(maskf, dt * a_row)
    cum_t = _dot_hi_tn(cum, eye)
    dt_t = _dot_hi_tn(dt, eye)
    cumx = _dot_hi(cum, expand)
    dtx = _dot_hi(dt, expand)
    rid = _iota2((C, 1), 0)
    lane_lo = _iota2((C, 2 * SSM_P), 1) < SSM_P
    for g in range(SSM_G):
        gs = slice(g * GD, (g + 1) * GD)
        bm = act_ref[rs, SSM_DI + g * SSM_N:SSM_DI + (g + 1) * SSM_N].astype(bf16)
        cm = act_ref[rs, SSM_DI + SSM_G * SSM_N + g * SSM_N:SSM_DI + SSM_G * SSM_N + (g + 1) * SSM_N].astype(bf16)
        cb = _dot_nt(cm, bm)
        xs = act_ref[rs, gs]
        cumg = cumx[:, gs]
        y_int = None
        for (lo, hi, s_in, s_out) in segs:
            yi = _dot_nt(cm, s_in[gs, :].astype(bf16))
            y_int = yi if y_int is None else jnp.where(rid >= lo, yi, y_int)
        yg = y_int * jnp.exp(cumg) + xs * dx_row[:, gs]
        for pr in range(SSM_HPG // 2):
            h0 = g * SSM_HPG + 2 * pr
            ps = slice(2 * pr * SSM_P, (2 * pr + 2) * SSM_P)
            xp = xs[:, ps].astype(bf16)
            acc = yg[:, ps]
            for hh, keep in ((h0, lane_lo), (h0 + 1, ~lane_lo)):
                seg = cum[:, hh:hh + 1] - cum_t[hh:hh + 1, :]
                mix = cb * jnp.exp(jnp.where(mask, seg, -jnp.inf)) * dt_t[hh:hh + 1, :]
                acc = acc + _dot(mix.astype(bf16), jnp.where(keep, xp, jnp.zeros_like(xp)))
            y_ref[rs, g * GD + 2 * pr * SSM_P:g * GD + (2 * pr + 2) * SSM_P] = acc
        for (lo, hi, s_in, s_out) in segs:
            wend = jnp.exp(cumg[hi - 1:hi, :] - cumg) * dtx[:, gs]
            xw = xs * wend
            if len(segs) > 1:
                xw = jnp.where((rid >= lo) & (rid < hi), xw, 0.0)
            upd = _dot_tn(xw.astype(bf16), bm)
            for r in range(SSM_HPG):
                hh = g * SSM_HPG + r
                hs = slice(hh * SSM_P, (hh + 1) * SSM_P)
                dec = jnp.exp(cum_t[hh:hh + 1, hi - 1:hi])
                s_out[hs, :] = s_in[hs, :] * dec + upd[r * SSM_P:(r + 1) * SSM_P, :]


def _conv_silu(terms, cw_ref, cb_ref):
    acc = terms[0] * cw_ref[0:1, :]
    for j in range(1, 4):
        acc = acc + terms[j] * cw_ref[j:j + 1, :]
    return _silu(cb_ref[...] + acc)


def _ssd_prompt_chunk(act_scr, rs, dtc, a_row, dx_ref, y_scr, s_scr, tri):
    C = SSM_C
    cum = _cumsum_rows(dtc * a_row)
    e_cum = jnp.exp(cum)
    wend = jnp.exp(cum[C - 1:C, :] - cum) * dtc
    cum_t = jnp.transpose(cum)
    dt_t = jnp.transpose(dtc)
    dec_t = jnp.exp(cum_t[:, C - 1:C])
    lane_lo = _iota2((C, 2 * SSM_P), 1) < SSM_P
    for g in range(SSM_G):
        gs = slice(g * GD, (g + 1) * GD)
        bm = act_scr[rs, SSM_DI + g * SSM_N:SSM_DI + (g + 1) * SSM_N].astype(bf16)
        cm = act_scr[rs, SSM_DI + SSM_G * SSM_N + g * SSM_N:SSM_DI + SSM_G * SSM_N + (g + 1) * SSM_N].astype(bf16)
        cb = _dot_nt(cm, bm)
        y_int = _dot_nt(cm, s_scr[gs, :].astype(bf16))
        xw = []
        for pr in range(SSM_HPG // 2):
            h0 = g * SSM_HPG + 2 * pr
            cols = slice(g * GD + 2 * pr * SSM_P, g * GD + (2 * pr + 2) * SSM_P)
            xs = act_scr[rs, cols]
            e_pair = jnp.where(lane_lo, e_cum[:, h0:h0 + 1], e_cum[:, h0 + 1:h0 + 2])
            w_pair = jnp.where(lane_lo, wend[:, h0:h0 + 1], wend[:, h0 + 1:h0 + 2])
            acc = y_int[:, 2 * pr * SSM_P:(2 * pr + 2) * SSM_P] * e_pair + xs * dx_ref[:, cols]
            xp = xs.astype(bf16)
            for hh, keep in ((h0, lane_lo), (h0 + 1, ~lane_lo)):
                seg = cum[:, hh:hh + 1] - cum_t[hh:hh + 1, :]
                mix = cb * jnp.exp(jnp.where(tri, seg, -jnp.inf)) * dt_t[hh:hh + 1, :]
                acc = acc + _dot(mix.astype(bf16), jnp.where(keep, xp, jnp.zeros_like(xp)))
            y_scr[rs, cols] = acc
            xw.append((xs * w_pair).astype(bf16))
        upd = _dot_tn(jnp.concatenate(xw, axis=1), bm)
        for r in range(SSM_HPG):
            hh = g * SSM_HPG + r
            hs = slice(hh * SSM_P, (hh + 1) * SSM_P)
            s_scr[hs, :] = s_scr[hs, :] * dec_t[hh:hh + 1, :] + upd[r * SSM_P:(r + 1) * SSM_P, :]


def _ssd_post_body(y, z, x, gn_ref, wout_ref):
    y = y * _silu(z)
    parts = []
    for g in range(SSM_G):
        gs = slice(g * GD, (g + 1) * GD)
        parts.append(_rms(y[:, gs], gn_ref[:, gs]))
    yn = jnp.concatenate(parts, axis=1).astype(bf16)
    return x + _dot(yn, wout_ref[...])


def _ssd_prompt_kernel(x_ref, g_ref, win_ref, wdt_ref, cw_ref, cb_ref, dtb_ref, alog_ref, dx_ref, gn_ref,
                       wout_ref, out_ref, sn_ref, cn_ref, ext_scr, act_scr, z_scr, y_scr, dt_scr, s_scr, *, nl):
    tm = x_ref.shape[0]
    C = SSM_C
    l = pl.program_id(1)

    @pl.when(l == 0)
    def _():
        ext_scr[0:8, :] = jnp.zeros((8, SSM_CONV_DIM), f32)
        s_scr[...] = jnp.zeros_like(s_scr)

    x = x_ref[...]
    h = _rms(x, g_ref[...]).astype(bf16)
    z_scr[...] = _dot(h, win_ref[:, 0:SSM_DI])
    ext_scr[8:8 + tm, :] = _dot(h, win_ref[:, SSM_DI:SSM_DI + SSM_CONV_DIM])
    dt_scr[...] = jax.nn.softplus(_dot(h, wdt_ref[...]) + dtb_ref[...])
    e = ext_scr[...]
    acc = e * cw_ref[0:1, :]
    for j in range(1, 4):
        acc = pltpu.roll(acc, 1, axis=0) + e * cw_ref[j:j + 1, :]
    act_scr[...] = _silu(cb_ref[...] + acc[8:8 + tm, :])
    a_row = -jnp.exp(alog_ref[...])
    tri = _iota2((C, C), 0) >= _iota2((C, C), 1)
    for c in range(tm // C):
        rs = slice(c * C, (c + 1) * C)
        _ssd_prompt_chunk(act_scr, rs, dt_scr[rs, :], a_row, dx_ref, y_scr, s_scr, tri)
    out_ref[...] = _ssd_post_body(y_scr[...], z_scr[...], x, gn_ref, wout_ref)

    @pl.when(l == nl - 1)
    def _():
        sn_ref[0] = s_scr[...]
        cn_ref[0] = ext_scr[8 + tm - 3:8 + tm, :]

    ext_scr[0:8, :] = ext_scr[tm:tm + 8, :]


def _ssd_prompt(x, g, win, wdt, cw, cb, dtb, alog, dx, gn, wout, nb, tm=256):
    T = x.shape[0]
    nl = T // nb // tm
    WIN = win.shape[1]
    row = pl.BlockSpec((tm, D), lambda b, l: (b * nl + l, 0))
    return pl.pallas_call(
        functools.partial(_ssd_prompt_kernel, nl=nl),
        out_shape=(jax.ShapeDtypeStruct((T, D), f32),
                   jax.ShapeDtypeStruct((nb, SSM_DI, SSM_N), f32),
                   jax.ShapeDtypeStruct((nb, 3, SSM_CONV_DIM), f32)),
        grid=(nb, nl),
        in_specs=[row, _cspec((1, D)), _cspec((D, WIN)), _cspec((D, 128)),
                  _cspec((4, SSM_CONV_DIM)), _cspec((1, SSM_CONV_DIM)), _cspec((1, 128)),
                  _cspec((1, 128)), _cspec((1, SSM_DI)), _cspec((1, SSM_DI)), _cspec((SSM_DI, D))],
        out_specs=(row,
                   pl.BlockSpec((1, SSM_DI, SSM_N), lambda b, l: (b, 0, 0)),
                   pl.BlockSpec((1, 3, SSM_CONV_DIM), lambda b, l: (b, 0, 0))),
        scratch_shapes=[pltpu.VMEM((tm + 8, SSM_CONV_DIM), f32), pltpu.VMEM((tm, SSM_CONV_DIM), f32),
                        pltpu.VMEM((tm, SSM_DI), f32), pltpu.VMEM((tm, SSM_DI), f32),
                        pltpu.VMEM((tm, 128), f32), pltpu.VMEM((SSM_DI, SSM_N), f32)],
        compiler_params=_params("arbitrary", "arbitrary"),
        name="ssd_prompt",
    )(x, g.reshape(1, D), win, wdt, cw, cb, dtb, alog, dx, gn.reshape(1, SSM_DI), wout)


def _ssd_sample_kernel(xbc_ref, dtr_ref, cs_ref, s0_ref, cw_ref, cb_ref, dtb_ref, alog_ref, dx_ref,
                       sel_ref, y_ref, sn_ref, act_scr, dt_scr):
    rows = xbc_ref.shape[0]
    xb = xbc_ref[...]
    t = _iota2((rows, 1), 0) & (DEC_SEQ - 1)
    parts = _split3(cs_ref[...])
    terms = []
    for j in range(4):
        sh = 3 - j
        term = xb if sh == 0 else jnp.where(t >= sh, pltpu.roll(xb, sh, axis=0), 0.0)
        if j < 3:
            term = term + _sel_dot(sel_ref[j], parts)
        terms.append(term)
    act_scr[...] = _conv_silu(terms, cw_ref, cb_ref)
    dt_scr[...] = jax.nn.softplus(dtr_ref[...] + dtb_ref[...])
    a_row = -jnp.exp(alog_ref[...])
    dx_row = dx_ref[...]
    r8 = _iota2((8, 8), 0)
    c8 = _iota2((8, 8), 1)
    mask8 = ((r8 >> 2) == (c8 >> 2)) & (r8 >= c8)

    def body(p, carry):
        rs = pl.ds(pl.multiple_of(p * 8, 8), 8)
        segs = [(0, 4, s0_ref.at[2 * p], sn_ref.at[2 * p]),
                (4, 8, s0_ref.at[2 * p + 1], sn_ref.at[2 * p + 1])]
        _ssd_chunk(act_scr, rs, dt_scr[rs, :], a_row, dx_row, y_ref, mask8, segs, 8)
        return carry

    lax.fori_loop(0, rows // 8, body, 0)


def _ssd_sample(xbc, dtr, cs2d, s0, cw, cb, dtb, alog, dx, sel):
    T = xbc.shape[0]
    nb = s0.shape[0]
    rb = _SB * DEC_SEQ
    cbk = _SB * 3
    st = pl.BlockSpec((_SB, SSM_DI, SSM_N), lambda i: (i, 0, 0))
    return pl.pallas_call(
        _ssd_sample_kernel,
        out_shape=(jax.ShapeDtypeStruct((T, SSM_DI), f32), jax.ShapeDtypeStruct((nb, SSM_DI, SSM_N), f32)),
        grid=(T // rb,),
        in_specs=[pl.BlockSpec((rb, SSM_CONV_DIM), lambda i: (i, 0)), pl.BlockSpec((rb, SSM_H), lambda i: (i, 0)),
                  pl.BlockSpec((cbk, SSM_CONV_DIM), lambda i: (i, 0)), st,
                  _cspec((4, SSM_CONV_DIM)), _cspec((1, SSM_CONV_DIM)), _cspec((1, SSM_H)),
                  _cspec((1, SSM_H)), _cspec((1, SSM_DI)), _cspec((3, rb, cbk))],
        out_specs=(pl.BlockSpec((rb, SSM_DI), lambda i: (i, 0)), st),
        scratch_shapes=[pltpu.VMEM((rb, SSM_CONV_DIM), f32), pltpu.VMEM((rb, SSM_H), f32)],
        compiler_params=_params("arbitrary"),
        name="ssd_sample",
    )(xbc, dtr, cs2d, s0, cw, cb, dtb, alog, dx, sel)


def _ssd_post_kernel(y_ref, z_ref, x_ref, gn_ref, wout_ref, out_ref):
    out_ref[...] = _ssd_post_body(y_ref[...], z_ref[...], x_ref[...], gn_ref, wout_ref)


def _ssd_post(y, z, x, gn, wout, tm=512):
    T = x.shape[0]
    row = pl.BlockSpec((tm, D), lambda i: (i, 0))
    wide = pl.BlockSpec((tm, SSM_DI), lambda i: (i, 0))
    return pl.pallas_call(
        _ssd_post_kernel,
        out_shape=jax.ShapeDtypeStruct((T, D), f32),
        grid=(T // tm,),
        in_specs=[wide, wide, row, _cspec((1, SSM_DI)), _cspec((SSM_DI, D))],
        out_specs=row,
        compiler_params=_params("arbitrary"),
        name="ssd_post",
    )(y, z, x, gn.reshape(1, SSM_DI), wout)


def _pool_selectors():
    sel = np.zeros((4, _SB * DEC_SEQ, _SB * POOL_HIST), np.float32)
    for gi, w in enumerate(POOL_WINDOWS):
        for s in range(_SB):
            for t in range(DEC_SEQ):
                for i in range(POOL_HIST):
                    if i >= POOL_HIST + 1 + t - w:
                        sel[gi, s * DEC_SEQ + t, s * POOL_HIST + i] = 1.0
    return sel


def _conv_selectors():
    sel = np.zeros((3, _SB * DEC_SEQ, _SB * 3), np.float32)
    for j in range(3):
        for s in range(_SB):
            for t in range(DEC_SEQ):
                if t + j < 3:
                    sel[j, s * DEC_SEQ + t, s * 3 + t + j] = 1.0
    return sel


def kernel(x_prompt, x_sample, state_pool_l1, state_gla_l2, state_ssm_l3, state_conv_l3, p_prompt, p_sample, norm_ffn1, ffn1_gate, ffn1_up, ffn1_down, norm_mix, norm_ffn2, ffn2_gate, ffn2_up, ffn2_down, norm_ple, ple_gate, ple_proj, norm_final, gm_w_in, gm_ln, gm_w_s, gm_b_s, gm_w_out, pool_w, pool_scale, gla_w_in, gla_w_a1, gla_w_a2, gla_b_a, gla_norm, gla_w_out, ssm_w_in, ssm_conv_w, ssm_conv_b, ssm_dt_bias, ssm_a_log, ssm_d, ssm_norm, ssm_w_out):
    nbp, seq, _ = x_prompt.shape
    nbs, dseq, _ = x_sample.shape
    assert dseq == DEC_SEQ and seq % 512 == 0 and nbs % _GLA_SB == 0
    cast = lambda w: w.astype(bf16)

    w1g, w1u, w1d = cast(ffn1_gate), cast(ffn1_up), cast(ffn1_down)
    w2g, w2u, w2d = cast(ffn2_gate), cast(ffn2_up), cast(ffn2_down)
    wpg, wpp = cast(ple_gate), cast(ple_proj)
    gm_in, gm_out = cast(gm_w_in), cast(gm_w_out)
    pw = cast(pool_w)
    gla_in, gla_a1, gla_a2, gla_out = cast(gla_w_in), cast(gla_w_a1), cast(gla_w_a2), cast(gla_w_out)
    ssm_in, ssm_out = cast(ssm_w_in), cast(ssm_w_out)
    lane_pad = lambda v: jnp.pad(v, [(0, 0)] * (v.ndim - 1) + [(0, 128 - SSM_H)])
    ssm_wdt = lane_pad(cast(ssm_w_in[:, SSM_DI + SSM_CONV_DIM:]))
    n1g, n2g, npg = (n.reshape(DEPTH, 1, D) for n in (norm_ffn1, norm_ffn2, norm_ple))

    dx = jnp.repeat(ssm_d, SSM_P).reshape(1, SSM_DI)
    dtb = ssm_dt_bias.reshape(1, SSM_H)
    alog = ssm_a_log.reshape(1, SSM_H)
    cbias = ssm_conv_b.reshape(1, SSM_CONV_DIM)
    bst = jnp.transpose(gm_b_s)

    t8 = np.arange(8) % DEC_SEQ
    wj = []
    for j in range(DEC_SEQ):
        src = t8 - j
        vals = gm_w_s[:, t8, np.maximum(src, 0)]
        vals = jnp.where(jnp.asarray(src >= 0)[None, :], vals, 0.0)
        wj.append(jnp.repeat(jnp.transpose(vals), 128, axis=1))
    wj = jnp.stack(wj)
    bt = jnp.repeat(jnp.transpose(gm_b_s[:, t8]), 128, axis=1)

    pool_sel = jnp.asarray(_pool_selectors(), bf16)
    conv_sel = jnp.asarray(_conv_selectors(), bf16)

    xp = x_prompt.reshape(nbp * seq, D)
    xs = x_sample.reshape(nbs * dseq, D)
    pp = p_prompt.reshape(DEPTH, nbp * seq, D_PLE)
    ps = p_sample.reshape(DEPTH, nbs * dseq, D_PLE)
    hist2d = state_pool_l1.reshape(nbs * POOL_HIST, D)
    cs2d = state_conv_l3.reshape(nbs * 3, SSM_CONV_DIM)
    ssm_s0 = state_ssm_l3.reshape(nbs, SSM_DI, SSM_N)

    outs = {}
    for i in range(DEPTH):
        xp, xs = _ffn(xp, xs, i, n1g, w1g, w1u, w1d)
        gmix = norm_mix[i]
        if i == 0:
            xp = _gmlp_prompt(xp, gmix, gm_in, gm_ln, gm_w_s, bst, gm_out)
            xs, chunk_v = _gmlp_sample(xs, gmix, gm_in, gm_ln, wj, bt, gm_out)
            outs["chunk_v"] = chunk_v.reshape(nbs, dseq, D)
        elif i == 1:
            xp, outs["pool_p"] = _pool_prompt(xp, gmix, pw, pool_scale, nbp)
            xs, hs = _pool_sample(xs, hist2d, gmix, pw, pool_scale, pool_sel, PAST_LEN)
            outs["pool_s"] = jnp.concatenate([state_pool_l1[:, dseq:], hs.reshape(nbs, dseq, D)], axis=1)
        elif i == 2:
            xp, outs["gla_p"] = _gla_prompt(xp, gmix, gla_in, gla_a1, gla_a2, gla_b_a, gla_norm, gla_out, nbp)
            qkvr, la = _gla_pre(xs, gmix, gla_in, gla_a1, gla_a2, gla_b_a)
            o, outs["gla_s"] = _gla_sample(qkvr, la, state_gla_l2)
            xs = _gla_post(o, qkvr, xs, gla_norm, gla_out)
        else:
            xp, sn, outs["conv_p"] = _ssd_prompt(xp, gmix, ssm_in, ssm_wdt, ssm_conv_w, cbias, lane_pad(dtb),
                                                 lane_pad(alog), dx, ssm_norm, ssm_out, nbp)
            outs["ssm_p"] = sn.reshape(nbp, SSM_H, SSM_P, SSM_N)
            z, xbc, dtr = _ssd_pre(xs, gmix, ssm_in)
            y, sn = _ssd_sample(xbc, dtr, cs2d, ssm_s0, ssm_conv_w, cbias, dtb, alog, dx, conv_sel)
            outs["ssm_s"] = sn.reshape(nbs, SSM_H, SSM_P, SSM_N)
            outs["conv_s"] = xbc.reshape(nbs, dseq, SSM_CONV_DIM)[:, dseq - 3:]
            xs = _ssd_post(y, z, xs, ssm_norm, ssm_out)
        fin = norm_final if i == DEPTH - 1 else None
        xp, xs = _ffn(xp, xs, i, n2g, w2g, w2u, w2d, ple=(pp, ps, npg, wpg, wpp), final_g=fin)

    return (xp.reshape(nbp, seq, D), xs.reshape(nbs, dseq, D), outs["chunk_v"], outs["pool_p"], outs["pool_s"],
            outs["gla_p"], outs["gla_s"], outs["ssm_p"], outs["ssm_s"], outs["conv_p"], outs["conv_s"])
```

```python
import functools

import jax
import jax.numpy as jnp
import numpy as np
from jax import lax
from jax.experimental import pallas as pl
from jax.experimental.pallas import tpu as pltpu

f32 = jnp.float32
bf16 = jnp.bfloat16

D = 1024
F = 2816
FC = 256
NF = F // FC
D_PLE = 256
EPS = 1e-6
DEPTH = 4

GM_CHUNK = 128
GM_GROUPS = 8
POOL_WINDOWS = (2, 4, 8, 16)
POOL_GD = D // 4
POOL_HIST = 15

GLA_H = 4
GLA_DK = 128
GLA_DV = 256
GLA_QK = GLA_H * GLA_DK
GLA_V = GLA_H * GLA_DV
GLA_NORMALIZER = 16.0
GLA_C = 64

SSM_DI = 2048
SSM_P = 64
SSM_H = 32
SSM_G = 4
SSM_HPG = 8
SSM_N = 128
SSM_CONV_DIM = SSM_DI + 2 * SSM_G * SSM_N
SSM_C = 128
GD = SSM_HPG * SSM_P

DEC_SEQ = 4
PAST_LEN = 16384

VMEM_LIMIT = 56 * 1024 * 1024


def _dot(a, b):
    return jnp.dot(a, b, preferred_element_type=f32)


def _dot_nt(a, b):
    return lax.dot_general(a, b, (((1,), (1,)), ((), ())), preferred_element_type=f32)


def _dot_tn(a, b):
    return lax.dot_general(a, b, (((0,), (0,)), ((), ())), preferred_element_type=f32)


def _dot_hi(a, b):
    return jnp.dot(a, b, precision=lax.Precision.HIGHEST, preferred_element_type=f32)


def _split3(x):
    hi = x.astype(bf16)
    r1 = x - hi.astype(f32)
    mid = r1.astype(bf16)
    lo = (r1 - mid.astype(f32)).astype(bf16)
    return hi, mid, lo


def _sel_dot(sel, parts):
    hi, mid, lo = parts
    return _dot(sel, hi) + _dot(sel, mid) + _dot(sel, lo)


def _rms(x, g):
    return x * lax.rsqrt(jnp.mean(x * x, axis=-1, keepdims=True) + EPS) * g


def _sigmoid(x):
    return 0.5 + 0.5 * jnp.tanh(0.5 * x)


def _silu(x):
    hx = 0.5 * x
    return hx + hx * jnp.tanh(hx)


def _cspec(shape):
    nd = len(shape)
    return pl.BlockSpec(shape, lambda *_: (0,) * nd, pipeline_mode=pl.Buffered(1))


def _params(*sem):
    return pltpu.CompilerParams(dimension_semantics=tuple(sem), vmem_limit_bytes=VMEM_LIMIT)


def _iota2(shape, dim):
    return lax.broadcasted_iota(jnp.int32, shape, dim)


def _ffn_kernel(*refs, ple, final, n_prompt):
    xp_ref, xs_ref, g_ref, wg_ref, wu_ref, wd_ref = refs[:6]
    k = 6
    if ple:
        pp_ref, ps_ref, gp_ref, wpg_ref, wpp_ref = refs[k:k + 5]
        k += 5
    if final:
        gf_ref = refs[k]
        k += 1
    op_ref, os_ref, h_scr, acc_scr = refs[k:k + 4]

    is_prompt = pl.program_id(0) < n_prompt
    x = jnp.where(is_prompt, xp_ref[...], xs_ref[...])
    h_scr[...] = _rms(x, g_ref[...]).astype(bf16)
    for j in range(NF):
        sl = slice(j * FC, (j + 1) * FC)
        hb = h_scr[...]
        a = _silu(_dot(hb, wg_ref[:, sl])) * _dot(hb, wu_ref[:, sl])
        contrib = _dot(a.astype(bf16), wd_ref[sl, :])
        if j == 0:
            acc_scr[...] = contrib
        else:
            acc_scr[...] += contrib
    y = x + 0.5 * acc_scr[...]
    if ple:
        h2 = _rms(y, gp_ref[...]).astype(bf16)
        gate = _sigmoid(_dot(h2, wpg_ref[...]))
        p = jnp.where(is_prompt, pp_ref[...], ps_ref[...]).astype(bf16)
        y = y + gate * _dot(p, wpp_ref[...])
    if final:
        y = _rms(y, gf_ref[...])

    @pl.when(is_prompt)
    def _():
        op_ref[...] = y

    @pl.when(jnp.logical_not(is_prompt))
    def _():
        os_ref[...] = y


def _lspec(shape, layer):
    nd = len(shape)
    return pl.BlockSpec((None,) + tuple(shape), lambda *_: (layer,) + (0,) * nd,
                        pipeline_mode=pl.Buffered(1))


def _ffn(xp, xs, layer, g, wg, wu, wd, ple=None, final_g=None):
    tm = xs.shape[0]
    n_prompt = xp.shape[0] // tm
    prow = lambda w: pl.BlockSpec((tm, w), lambda i: (jnp.minimum(i, n_prompt - 1), 0))
    srow = lambda w: pl.BlockSpec((tm, w), lambda i: (0, 0))
    in_specs = [prow(D), srow(D), _lspec((1, D), layer), _lspec((D, F), layer), _lspec((D, F), layer),
                _lspec((F, D), layer)]
    args = [xp, xs, g, wg, wu, wd]
    if ple is not None:
        pp, ps, gp, wpg, wpp = ple
        in_specs += [pl.BlockSpec((None, tm, D_PLE), lambda i: (layer, jnp.minimum(i, n_prompt - 1), 0)),
                     pl.BlockSpec((None, tm, D_PLE), lambda i: (layer, 0, 0)),
                     _lspec((1, D), layer), _lspec((D, D), layer), _lspec((D_PLE, D), layer)]
        args += [pp, ps, gp, wpg, wpp]
    if final_g is not None:
        in_specs.append(_cspec((1, D)))
        args.append(final_g.reshape(1, D))
    return pl.pallas_call(
        functools.partial(_ffn_kernel, ple=ple is not None, final=final_g is not None, n_prompt=n_prompt),
        out_shape=(jax.ShapeDtypeStruct(xp.shape, f32), jax.ShapeDtypeStruct(xs.shape, f32)),
        grid=(n_prompt + 1,),
        in_specs=in_specs,
        out_specs=(prow(D), srow(D)),
        scratch_shapes=[pltpu.VMEM((tm, D), bf16), pltpu.VMEM((tm, D), f32)],
        compiler_params=_params("arbitrary"),
        name="ffn",
    )(*args)


def _gmlp_act(y, ln_ref):
    y = jax.nn.gelu(y, approximate=True)
    u = y[:, :D]
    v = y[:, D:]
    vc = v - jnp.mean(v, axis=-1, keepdims=True)
    v = vc * lax.rsqrt(jnp.mean(vc * vc, axis=-1, keepdims=True) + EPS) * ln_ref[...]
    return u, v


def _gmlp_pre(x, g, win_ref, ln_ref):
    h = _rms(x, g).astype(bf16)
    return _gmlp_act(_dot(h, win_ref[...]), ln_ref)


def _two_stage(step, bufs0, bufs1):
    par = lax.rem(pl.program_id(0), 2)

    @pl.when(par == 0)
    def _():
        step(bufs0, bufs1)

    @pl.when(par == 1)
    def _():
        step(bufs1, bufs0)


def _stage_specs(tm, nt):
    a = pl.BlockSpec((tm, D), lambda s: (jnp.minimum(s, nt - 1), 0))
    b = pl.BlockSpec((tm, D), lambda s: (jnp.maximum(s - 1, 0), 0))
    return a, b


def _gmlp_prompt_kernel(xa_ref, xb_ref, g_ref, win_ref, ln_ref, ws_ref, bst_ref, wout_ref, o_ref,
                        y0, y1, u_scr, vb_scr, t_scr):
    tm = xa_ref.shape[0]

    @pl.when(pl.program_id(0) == 0)
    def _():
        y1[...] = jnp.zeros_like(y1)

    def step(yw, yr):
        yw[...] = _dot(_rms(xa_ref[...], g_ref[...]).astype(bf16), win_ref[...])
        u, v = _gmlp_act(yr[...], ln_ref)
        u_scr[...] = u
        vb_scr[...] = v.astype(bf16)
        causal = _iota2((GM_CHUNK, GM_CHUNK), 0) >= _iota2((GM_CHUNK, GM_CHUNK), 1)
        for g in range(GM_GROUPS):
            wm = jnp.where(causal, ws_ref[g], 0.0).astype(bf16)
            bias = bst_ref[:, g:g + 1]
            cs = slice(g * 128, (g + 1) * 128)
            for ch in range(tm // GM_CHUNK):
                rs = slice(ch * GM_CHUNK, (ch + 1) * GM_CHUNK)
                sv = _dot(wm, vb_scr[rs, cs]) + bias
                t_scr[rs, cs] = (u_scr[rs, cs] * sv).astype(bf16)
        o_ref[...] = xb_ref[...] + _dot(t_scr[...], wout_ref[...])

    _two_stage(step, y0, y1)


def _gmlp_prompt(x, g, win, ln, ws, bst, wout, tm=512):
    T = x.shape[0]
    nt = T // tm
    a, b = _stage_specs(tm, nt)
    return pl.pallas_call(
        _gmlp_prompt_kernel,
        out_shape=jax.ShapeDtypeStruct((T, D), f32),
        grid=(nt + 1,),
        in_specs=[a, b, _cspec((1, D)), _cspec((D, 2 * D)), _cspec((1, D)),
                  _cspec((GM_GROUPS, GM_CHUNK, GM_CHUNK)), _cspec((GM_CHUNK, GM_GROUPS)),
                  _cspec((D, D))],
        out_specs=b,
        scratch_shapes=[pltpu.VMEM((tm, 2 * D), f32), pltpu.VMEM((tm, 2 * D), f32),
                        pltpu.VMEM((tm, D), f32), pltpu.VMEM((tm, D), bf16), pltpu.VMEM((tm, D), bf16)],
        compiler_params=_params("arbitrary"),
        name="gmlp_prompt",
    )(x, x, g.reshape(1, D), win, ln.reshape(1, D), ws, bst, wout)


def _gmlp_sample_kernel(x_ref, g_ref, win_ref, ln_ref, wj_ref, bt_ref, wout_ref, o_ref, v_ref):
    rows = x_ref.shape[0]
    n = rows // 8
    x = x_ref[...]
    u, v = _gmlp_pre(x, g_ref[...], win_ref, ln_ref)
    v_ref[...] = v
    acc = v.reshape(n, 8, D) * wj_ref[0][None]
    for j in range(1, DEC_SEQ):
        acc = acc + pltpu.roll(v, j, axis=0).reshape(n, 8, D) * wj_ref[j][None]
    sv = (acc + bt_ref[...][None]).reshape(rows, D)
    o_ref[...] = x + _dot((u * sv).astype(bf16), wout_ref[...])


def _gmlp_sample(x, g, win, ln, wj, bt, wout):
    T = x.shape[0]
    full = pl.BlockSpec((T, D), lambda i: (0, 0))
    return pl.pallas_call(
        _gmlp_sample_kernel,
        out_shape=(jax.ShapeDtypeStruct((T, D), f32), jax.ShapeDtypeStruct((T, D), f32)),
        grid=(1,),
        in_specs=[full, _cspec((1, D)), _cspec((D, 2 * D)), _cspec((1, D)),
                  _cspec((DEC_SEQ, 8, D)), _cspec((8, D)), _cspec((D, D))],
        out_specs=(full, full),
        compiler_params=_params("arbitrary"),
        name="gmlp_sample",
    )(x, g.reshape(1, D), win, ln.reshape(1, D), wj, bt, wout)


def _pool_prompt_kernel(x_ref, g_ref, pw_ref, sc_ref, o_ref, pn_ref, ext_scr, *, nl):
    tm = x_ref.shape[0]
    l = pl.program_id(1)

    @pl.when(l == 0)
    def _():
        ext_scr[0:16, :] = jnp.zeros((16, D), f32)

    x = x_ref[...]
    ext_scr[16:16 + tm, :] = _rms(x, g_ref[...])
    pos = l * tm + _iota2((tm, 1), 0)
    for gi, w in enumerate(POOL_WINDOWS):
        cs = slice(gi * POOL_GD, (gi + 1) * POOL_GD)
        h = ext_scr[16:16 + tm, cs]
        s = h
        for j in range(1, w):
            s = s + ext_scr[16 - j:16 - j + tm, cs]
        cnt = jnp.minimum(pos + 1, w).astype(f32)
        diff = (s / cnt - h).astype(bf16)
        o_ref[:, cs] = x[:, cs] + _dot(diff, pw_ref[gi]) * sc_ref[:, cs]

    @pl.when(l == nl - 1)
    def _():
        pn_ref[0] = ext_scr[16 + tm - POOL_HIST:16 + tm, :]

    ext_scr[0:16, :] = ext_scr[tm:tm + 16, :]


def _pool_prompt(x, g, pw, sc, nb, tm=512):
    T = x.shape[0]
    nl = T // nb // tm
    row = pl.BlockSpec((tm, D), lambda b, l: (b * nl + l, 0))
    return pl.pallas_call(
        functools.partial(_pool_prompt_kernel, nl=nl),
        out_shape=(jax.ShapeDtypeStruct((T, D), f32), jax.ShapeDtypeStruct((nb, POOL_HIST, D), f32)),
        grid=(nb, nl),
        in_specs=[row, _cspec((1, D)), _cspec((4, POOL_GD, POOL_GD)), _cspec((1, D))],
        out_specs=(row, pl.BlockSpec((1, POOL_HIST, D), lambda b, l: (b, 0, 0))),
        scratch_shapes=[pltpu.VMEM((tm + 16, D), f32)],
        compiler_params=_params("arbitrary", "arbitrary"),
        name="pool_prompt",
    )(x, g.reshape(1, D), pw, sc.reshape(1, D))


_SB = 8


def _pool_sample_kernel(x_ref, hist_ref, g_ref, pw_ref, sc_ref, sel_ref, o_ref, h_ref, *, pos0):
    rows = x_ref.shape[0]
    x = x_ref[...]
    h = _rms(x, g_ref[...])
    h_ref[...] = h
    t = _iota2((rows, 1), 0) & (DEC_SEQ - 1)
    parts = _split3(hist_ref[...])
    for gi, w in enumerate(POOL_WINDOWS):
        cs = slice(gi * POOL_GD, (gi + 1) * POOL_GD)
        hc = h[:, cs]
        s = hc
        for j in range(1, min(w, DEC_SEQ)):
            s = s + jnp.where(t >= j, pltpu.roll(hc, j, axis=0), 0.0)
        s = s + _sel_dot(sel_ref[gi], tuple(p[:, cs] for p in parts))
        cnt = jnp.minimum(pos0 + t + 1, w).astype(f32)
        diff = (s / cnt - hc).astype(bf16)
        o_ref[:, cs] = x[:, cs] + _dot(diff, pw_ref[gi]) * sc_ref[:, cs]


def _pool_sample(x, hist2d, g, pw, sc, sel, pos0):
    T = x.shape[0]
    rb = _SB * DEC_SEQ
    hb = _SB * POOL_HIST
    row = pl.BlockSpec((rb, D), lambda i: (i, 0))
    return pl.pallas_call(
        functools.partial(_pool_sample_kernel, pos0=pos0),
        out_shape=(jax.ShapeDtypeStruct((T, D), f32), jax.ShapeDtypeStruct((T, D), f32)),
        grid=(T // rb,),
        in_specs=[row, pl.BlockSpec((hb, D), lambda i: (i, 0)), _cspec((1, D)),
                  _cspec((4, POOL_GD, POOL_GD)), _cspec((1, D)), _cspec((4, rb, hb))],
        out_specs=(row, row),
        compiler_params=_params("arbitrary"),
        name="pool_sample",
    )(x, hist2d, g.reshape(1, D), pw, sc.reshape(1, D), sel)


def _cumsum_rows(x, seg=None):
    n = x.shape[0]
    seg = n if seg is None else seg
    pos = _iota2((n, 1), 0) & (seg - 1)
    s = 1
    while s < seg:
        x = x + jnp.where(pos >= s, pltpu.roll(x, s, axis=0), 0.0)
        s *= 2
    return x


def _gla_pre_body(x, g_ref, win_ref, wa1_ref, wa2_ref, ba_ref, qkvr_ref, la_ref):
    h = _rms(x, g_ref[...]).astype(bf16)
    qkvr_ref[...] = _dot(h, win_ref[...])
    a1 = _dot(h, wa1_ref[...]).astype(bf16)
    z = _dot(a1, wa2_ref[...]) + ba_ref[...]
    la_ref[...] = jax.nn.log_sigmoid(z) / GLA_NORMALIZER


def _gla_post_body(o, r, x, gn_ref, wout_ref):
    parts = []
    for hd in range(GLA_H):
        vs = slice(hd * GLA_DV, (hd + 1) * GLA_DV)
        parts.append(_rms(o[:, vs], gn_ref[:, vs]))
    on = jnp.concatenate(parts, axis=1)
    gated = (on * _silu(r)).astype(bf16)
    return x + _dot(gated, wout_ref[...])


def _gla_pre_kernel(x_ref, g_ref, win_ref, wa1_ref, wa2_ref, ba_ref, qkvr_ref, la_ref):
    _gla_pre_body(x_ref[...], g_ref, win_ref, wa1_ref, wa2_ref, ba_ref, qkvr_ref, la_ref)


def _gla_pre(x, g, win, wa1, wa2, ba, tm=512):
    T = x.shape[0]
    W = 2 * GLA_QK + 2 * GLA_V
    return pl.pallas_call(
        _gla_pre_kernel,
        out_shape=(jax.ShapeDtypeStruct((T, W), f32), jax.ShapeDtypeStruct((T, GLA_QK), f32)),
        grid=(T // tm,),
        in_specs=[pl.BlockSpec((tm, D), lambda i: (i, 0)), _cspec((1, D)), _cspec((D, W)),
                  _cspec((D, 16)), _cspec((16, GLA_QK)), _cspec((1, GLA_QK))],
        out_specs=(pl.BlockSpec((tm, W), lambda i: (i, 0)), pl.BlockSpec((tm, GLA_QK), lambda i: (i, 0))),
        compiler_params=_params("arbitrary"),
        name="gla_pre",
    )(x, g.reshape(1, D), win, wa1, wa2, ba.reshape(1, GLA_QK))


def _gla_chunk(q, k, v, b, bl, anc, mask, segs, C):
    qs = q * (GLA_DK ** -0.5)
    q_dec = (qs * jnp.exp(b)).astype(bf16)
    k_end = k * jnp.exp(bl - b)
    q_mid = (qs * jnp.exp(b - anc)).astype(bf16)
    k_mid = (k * jnp.exp(anc - b)).astype(bf16)
    ebl = jnp.exp(bl)
    vb = v.astype(bf16)
    rid = _iota2((C, 1), 0)
    outs = []
    for hd in range(GLA_H):
        ds_ = slice(hd * GLA_DK, (hd + 1) * GLA_DK)
        vs = slice(hd * GLA_DV, (hd + 1) * GLA_DV)
        sc = jnp.where(mask, _dot_nt(q_mid[:, ds_], k_mid[:, ds_]), 0.0)
        o = _dot(sc.astype(bf16), vb[:, vs])
        o_int = None
        for (lo, hi, s_in, s_out) in segs:
            oi = _dot(q_dec[:, ds_], s_in[hd].astype(bf16))
            o_int = oi if o_int is None else jnp.where(rid >= lo, oi, o_int)
        outs.append(o + o_int)
        for (lo, hi, s_in, s_out) in segs:
            if ebl.shape[0] == 1:
                dec = jnp.transpose(jnp.broadcast_to(ebl[:, ds_], (8, GLA_DK)))[:, 0:1]
            else:
                r0 = max(hi - 8, 0)
                dec = jnp.transpose(ebl[r0:r0 + 8, ds_])[:, hi - 1 - r0:hi - r0]
            ke = k_end[:, ds_]
            if len(segs) > 1:
                ke = jnp.where((rid >= lo) & (rid < hi), ke, 0.0)
            s_out[hd] = s_in[hd] * dec + _dot_tn(ke.astype(bf16), vb[:, vs])
    return jnp.concatenate(outs, axis=1)


def _gla_prompt_kernel(xa_ref, xb_ref, g_ref, win_ref, wa1_ref, wa2_ref, ba_ref, gn_ref, wout_ref,
                       out_ref, sn_ref, q0, l0, q1, l1, o_scr, s_scr, *, nl):
    tm = xa_ref.shape[0]
    C = GLA_C
    s = pl.program_id(0)
    tpos = lax.rem(s - 1, nl)

    @pl.when(s == 0)
    def _():
        q1[...] = jnp.zeros_like(q1)
        l1[...] = jnp.zeros_like(l1)

    @pl.when((s == 0) | (tpos == 0))
    def _():
        s_scr[...] = jnp.zeros_like(s_scr)

    def step(bw, br):
        qw, lw = bw
        qr, lr = br
        _gla_pre_body(xa_ref[...], g_ref, win_ref, wa1_ref, wa2_ref, ba_ref, qw, lw)
        tri = _iota2((C, C), 0) >= _iota2((C, C), 1)
        for c in range(tm // C):
            rs = slice(c * C, (c + 1) * C)
            b = _cumsum_rows(lr[rs, :])
            bl = b[C - 1:C, :]
            anc = b[C // 2 - 1:C // 2, :]
            q = qr[rs, 0:GLA_QK]
            k = qr[rs, GLA_QK:2 * GLA_QK]
            v = qr[rs, 2 * GLA_QK:2 * GLA_QK + GLA_V]
            o_scr[rs, :] = _gla_chunk(q, k, v, b, bl, anc, tri, [(0, C, s_scr, s_scr)], C)
        out_ref[...] = _gla_post_body(o_scr[...], qr[:, 2 * GLA_QK + GLA_V:], xb_ref[...], gn_ref, wout_ref)

    _two_stage(step, (q0, l0), (q1, l1))

    @pl.when(tpos == nl - 1)
    def _():
        sn_ref[0] = s_scr[...]


def _gla_prompt(x, g, win, wa1, wa2, ba, gn, wout, nb, tm=256):
    T = x.shape[0]
    nt = T // tm
    nl = nt // nb
    W = 2 * GLA_QK + 2 * GLA_V
    a, b = _stage_specs(tm, nt)
    return pl.pallas_call(
        functools.partial(_gla_prompt_kernel, nl=nl),
        out_shape=(jax.ShapeDtypeStruct((T, D), f32),
                   jax.ShapeDtypeStruct((nb, GLA_H, GLA_DK, GLA_DV), f32)),
        grid=(nt + 1,),
        in_specs=[a, b, _cspec((1, D)), _cspec((D, W)), _cspec((D, 16)), _cspec((16, GLA_QK)),
                  _cspec((1, GLA_QK)), _cspec((1, GLA_V)), _cspec((GLA_V, D))],
        out_specs=(b, pl.BlockSpec((1, GLA_H, GLA_DK, GLA_DV), lambda s: (jnp.maximum(s - 1, 0) // nl, 0, 0, 0))),
        scratch_shapes=[pltpu.VMEM((tm, W), f32), pltpu.VMEM((tm, GLA_QK), f32),
                        pltpu.VMEM((tm, W), f32), pltpu.VMEM((tm, GLA_QK), f32),
                        pltpu.VMEM((tm, GLA_V), f32), pltpu.VMEM((GLA_H, GLA_DK, GLA_DV), f32)],
        compiler_params=_params("arbitrary"),
        name="gla_prompt",
    )(x, x, g.reshape(1, D), win, wa1, wa2, ba.reshape(1, GLA_QK), gn.reshape(1, GLA_V), wout)


_GLA_SB = 16


def _gla_sample_kernel(qkv_ref, la_ref, s0_ref, o_ref, sn_ref, b_scr, bl_scr):
    rows = qkv_ref.shape[0]
    r = _iota2((rows, rows), 0)
    c = _iota2((rows, rows), 1)
    same = (r >> 2) == (c >> 2)
    la = la_ref[...]
    b_scr[...] = _dot_hi((same & (r >= c)).astype(f32), la)
    bl_scr[...] = _dot_hi(same.astype(f32), la)
    r8 = _iota2((8, 8), 0)
    c8 = _iota2((8, 8), 1)
    mask8 = ((r8 >> 2) == (c8 >> 2)) & (r8 >= c8)

    def body(p, carry):
        rs = pl.ds(pl.multiple_of(p * 8, 8), 8)
        q = qkv_ref[rs, 0:GLA_QK]
        k = qkv_ref[rs, GLA_QK:2 * GLA_QK]
        v = qkv_ref[rs, 2 * GLA_QK:2 * GLA_QK + GLA_V]
        segs = [(0, 4, s0_ref.at[2 * p], sn_ref.at[2 * p]),
                (4, 8, s0_ref.at[2 * p + 1], sn_ref.at[2 * p + 1])]
        o_ref[rs, :] = _gla_chunk(q, k, v, b_scr[rs, :], bl_scr[rs, :], 0.0, mask8, segs, 8)
        return carry

    lax.fori_loop(0, rows // 8, body, 0)


def _gla_sample(qkvr, la, s0):
    T = qkvr.shape[0]
    nb = s0.shape[0]
    rb = _GLA_SB * DEC_SEQ
    W = 2 * GLA_QK + GLA_V
    st = pl.BlockSpec((_GLA_SB, GLA_H, GLA_DK, GLA_DV), lambda i: (i, 0, 0, 0))
    return pl.pallas_call(
        _gla_sample_kernel,
        out_shape=(jax.ShapeDtypeStruct((T, GLA_V), f32),
                   jax.ShapeDtypeStruct((nb, GLA_H, GLA_DK, GLA_DV), f32)),
        grid=(T // rb,),
        in_specs=[pl.BlockSpec((rb, W), lambda i: (i, 0)), pl.BlockSpec((rb, GLA_QK), lambda i: (i, 0)), st],
        out_specs=(pl.BlockSpec((rb, GLA_V), lambda i: (i, 0)), st),
        scratch_shapes=[pltpu.VMEM((rb, GLA_QK), f32), pltpu.VMEM((rb, GLA_QK), f32)],
        compiler_params=_params("arbitrary"),
        name="gla_sample",
    )(qkvr, la, s0)


def _gla_post_kernel(o_ref, r_ref, x_ref, gn_ref, wout_ref, out_ref):
    out_ref[...] = _gla_post_body(o_ref[...], r_ref[...], x_ref[...], gn_ref, wout_ref)


def _gla_post(o, qkvr, x, gn, wout, tm=512):
    T = x.shape[0]
    row = pl.BlockSpec((tm, D), lambda i: (i, 0))
    return pl.pallas_call(
        _gla_post_kernel,
        out_shape=jax.ShapeDtypeStruct((T, D), f32),
        grid=(T // tm,),
        in_specs=[row, pl.BlockSpec((tm, GLA_V), lambda i: (i, 2)), row, _cspec((1, GLA_V)),
                  _cspec((GLA_V, D))],
        out_specs=row,
        compiler_params=_params("arbitrary"),
        name="gla_post",
    )(o, qkvr, x, gn.reshape(1, GLA_V), wout)


def _ssd_pre_kernel(x_ref, g_ref, win_ref, z_ref, xbc_ref, dt_ref):
    h = _rms(x_ref[...], g_ref[...]).astype(bf16)
    z_ref[...] = _dot(h, win_ref[:, 0:SSM_DI])
    xbc_ref[...] = _dot(h, win_ref[:, SSM_DI:SSM_DI + SSM_CONV_DIM])
    dt_ref[...] = _dot(h, win_ref[:, SSM_DI + SSM_CONV_DIM:])


def _ssd_pre(x, g, win, tm=512):
    T = x.shape[0]
    return pl.pallas_call(
        _ssd_pre_kernel,
        out_shape=(jax.ShapeDtypeStruct((T, SSM_DI), f32), jax.ShapeDtypeStruct((T, SSM_CONV_DIM), f32),
                   jax.ShapeDtypeStruct((T, SSM_H), f32)),
        grid=(T // tm,),
        in_specs=[pl.BlockSpec((tm, D), lambda i: (i, 0)), _cspec((1, D)), _cspec((D, win.shape[1]))],
        out_specs=(pl.BlockSpec((tm, SSM_DI), lambda i: (i, 0)),
                   pl.BlockSpec((tm, SSM_CONV_DIM), lambda i: (i, 0)),
                   pl.BlockSpec((tm, SSM_H), lambda i: (i, 0))),
        compiler_params=_params("arbitrary"),
        name="ssd_pre",
    )(x, g.reshape(1, D), win)


def _conv_silu(terms, cw_ref, cb_ref):
    acc = terms[0] * cw_ref[0:1, :]
    for j in range(1, 4):
        acc = acc + terms[j] * cw_ref[j:j + 1, :]
    return _silu(cb_ref[...] + acc)


def _ssd_chunk(act_ref, rs, dtc, a_row, dx_ref, y_ref, mask, segs, C):
    rid = _iota2((C, 1), 0)
    cum = _cumsum_rows(dtc * a_row, segs[0][1] - segs[0][0])
    cum_last = cum[segs[-1][1] - 1:segs[-1][1], :]
    for (lo, hi, _, _) in segs[-2::-1]:
        cum_last = jnp.where(rid < hi, cum[hi - 1:hi, :], cum_last)
    e_cum = jnp.exp(cum)
    wend = jnp.exp(cum_last - cum) * dtc
    cum_t = jnp.transpose(cum)
    dt_t = jnp.transpose(dtc)
    dec_t = jnp.exp(cum_t)
    lane_lo = _iota2((C, 2 * SSM_P), 1) < SSM_P
    for g in range(SSM_G):
        gs = slice(g * GD, (g + 1) * GD)
        bm = act_ref[rs, SSM_DI + g * SSM_N:SSM_DI + (g + 1) * SSM_N].astype(bf16)
        cm = act_ref[rs, SSM_DI + SSM_G * SSM_N + g * SSM_N:SSM_DI + SSM_G * SSM_N + (g + 1) * SSM_N].astype(bf16)
        cb = _dot_nt(cm, bm)
        y_int = None
        for (lo, hi, s_in, s_out) in segs:
            yi = _dot_nt(cm, s_in[gs, :].astype(bf16))
            y_int = yi if y_int is None else jnp.where(rid >= lo, yi, y_int)
        xw = []
        for pr in range(SSM_HPG // 2):
            h0 = g * SSM_HPG + 2 * pr
            cols = slice(g * GD + 2 * pr * SSM_P, g * GD + (2 * pr + 2) * SSM_P)
            xs = act_ref[rs, cols]
            e_pair = jnp.where(lane_lo, e_cum[:, h0:h0 + 1], e_cum[:, h0 + 1:h0 + 2])
            w_pair = jnp.where(lane_lo, wend[:, h0:h0 + 1], wend[:, h0 + 1:h0 + 2])
            acc = y_int[:, 2 * pr * SSM_P:(2 * pr + 2) * SSM_P] * e_pair + xs * dx_ref[:, cols]
            xp = xs.astype(bf16)
            for hh, keep in ((h0, lane_lo), (h0 + 1, ~lane_lo)):
                seg = cum[:, hh:hh + 1] - cum_t[hh:hh + 1, :]
                mix = cb * jnp.exp(jnp.where(mask, seg, -jnp.inf)) * dt_t[hh:hh + 1, :]
                acc = acc + _dot(mix.astype(bf16), jnp.where(keep, xp, jnp.zeros_like(xp)))
            y_ref[rs, cols] = acc
            xw.append(xs * w_pair)
        xw = jnp.concatenate(xw, axis=1)
        for (lo, hi, s_in, s_out) in segs:
            xws = xw if len(segs) == 1 else jnp.where((rid >= lo) & (rid < hi), xw, 0.0)
            upd = _dot_tn(xws.astype(bf16), bm)
            for r in range(SSM_HPG):
                hh = g * SSM_HPG + r
                hs = slice(hh * SSM_P, (hh + 1) * SSM_P)
                s_out[hs, :] = s_in[hs, :] * dec_t[hh:hh + 1, hi - 1:hi] + upd[r * SSM_P:(r + 1) * SSM_P, :]


def _ssd_post_body(y, z, x, gn_ref, wout_ref):
    y = y * _silu(z)
    parts = []
    for g in range(SSM_G):
        gs = slice(g * GD, (g + 1) * GD)
        parts.append(_rms(y[:, gs], gn_ref[:, gs]))
    yn = jnp.concatenate(parts, axis=1).astype(bf16)
    return x + _dot(yn, wout_ref[...])


def _ssd_prompt_kernel(xa_ref, xb_ref, g_ref, win_ref, wdt_ref, cw_ref, cb_ref, dtb_ref, alog_ref, dx_ref, gn_ref,
                       wout_ref, out_ref, sn_ref, cn_ref, z0, e0, d0, z1, e1, d1, carry_scr, act_scr, y_scr, s_scr,
                       *, nl):
    tm = xa_ref.shape[0]
    C = SSM_C
    s = pl.program_id(0)
    tpos = lax.rem(s - 1, nl)

    @pl.when(s == 0)
    def _():
        z1[...] = jnp.zeros_like(z1)
        e1[...] = jnp.zeros_like(e1)
        d1[...] = jnp.zeros_like(d1)

    @pl.when((s == 0) | (tpos == 0))
    def _():
        carry_scr[...] = jnp.zeros_like(carry_scr)
        s_scr[...] = jnp.zeros_like(s_scr)

    def step(bw, br):
        zw, ew, dw = bw
        zr, er, dr = br
        h = _rms(xa_ref[...], g_ref[...]).astype(bf16)
        zw[...] = _dot(h, win_ref[:, 0:SSM_DI])
        ew[8:8 + tm, :] = _dot(h, win_ref[:, SSM_DI:SSM_DI + SSM_CONV_DIM])
        dw[...] = _dot(h, wdt_ref[...])
        er[0:8, :] = carry_scr[...]
        e = er[...]
        acc = e * cw_ref[0:1, :]
        for j in range(1, 4):
            acc = pltpu.roll(acc, 1, axis=0) + e * cw_ref[j:j + 1, :]
        act_scr[...] = _silu(cb_ref[...] + acc[8:8 + tm, :])
        carry_scr[...] = er[tm:tm + 8, :]
        a_row = -jnp.exp(alog_ref[...])
        tri = _iota2((C, C), 0) >= _iota2((C, C), 1)
        for c in range(tm // C):
            rs = slice(c * C, (c + 1) * C)
            dtc = jax.nn.softplus(dr[rs, :] + dtb_ref[...])
            _ssd_chunk(act_scr, rs, dtc, a_row, dx_ref, y_scr, tri, [(0, C, s_scr, s_scr)], C)
        out_ref[...] = _ssd_post_body(y_scr[...], zr[...], xb_ref[...], gn_ref, wout_ref)

        @pl.when(tpos == nl - 1)
        def _():
            cn_ref[0] = er[8 + tm - 3:8 + tm, :]

    _two_stage(step, (z0, e0, d0), (z1, e1, d1))

    @pl.when(tpos == nl - 1)
    def _():
        sn_ref[0] = s_scr[...]


def _ssd_prompt(x, g, win, wdt, cw, cb, dtb, alog, dx, gn, wout, nb, tm=256):
    T = x.shape[0]
    nt = T // tm
    nl = nt // nb
    WIN = win.shape[1]
    a, b = _stage_specs(tm, nt)
    seq = lambda s: (jnp.maximum(s - 1, 0) // nl, 0, 0)
    bufs = [pltpu.VMEM((tm, SSM_DI), f32), pltpu.VMEM((tm + 8, SSM_CONV_DIM), f32), pltpu.VMEM((tm, 128), f32)]
    return pl.pallas_call(
        functools.partial(_ssd_prompt_kernel, nl=nl),
        out_shape=(jax.ShapeDtypeStruct((T, D), f32),
                   jax.ShapeDtypeStruct((nb, SSM_DI, SSM_N), f32),
                   jax.ShapeDtypeStruct((nb, 3, SSM_CONV_DIM), f32)),
        grid=(nt + 1,),
        in_specs=[a, b, _cspec((1, D)), _cspec((D, WIN)), _cspec((D, 128)),
                  _cspec((4, SSM_CONV_DIM)), _cspec((1, SSM_CONV_DIM)), _cspec((1, 128)),
                  _cspec((1, 128)), _cspec((1, SSM_DI)), _cspec((1, SSM_DI)), _cspec((SSM_DI, D))],
        out_specs=(b, pl.BlockSpec((1, SSM_DI, SSM_N), seq), pl.BlockSpec((1, 3, SSM_CONV_DIM), seq)),
        scratch_shapes=bufs + bufs + [pltpu.VMEM((8, SSM_CONV_DIM), f32), pltpu.VMEM((tm, SSM_CONV_DIM), f32),
                                      pltpu.VMEM((tm, SSM_DI), f32), pltpu.VMEM((SSM_DI, SSM_N), f32)],
        compiler_params=_params("arbitrary"),
        name="ssd_prompt",
    )(x, x, g.reshape(1, D), win, wdt, cw, cb, dtb, alog, dx, gn.reshape(1, SSM_DI), wout)


def _ssd_sample_kernel(xbc_ref, dtr_ref, cs_ref, s0_ref, cw_ref, cb_ref, dtb_ref, alog_ref, dx_ref,
                       sel_ref, y_ref, sn_ref, act_scr, dt_scr):
    rows = xbc_ref.shape[0]
    xb = xbc_ref[...]
    t = _iota2((rows, 1), 0) & (DEC_SEQ - 1)
    parts = _split3(cs_ref[...])
    terms = []
    for j in range(4):
        sh = 3 - j
        term = xb if sh == 0 else jnp.where(t >= sh, pltpu.roll(xb, sh, axis=0), 0.0)
        if j < 3:
            term = term + _sel_dot(sel_ref[j], parts)
        terms.append(term)
    act_scr[...] = _conv_silu(terms, cw_ref, cb_ref)
    dt_scr[...] = jnp.zeros_like(dt_scr)
    dt_scr[:, 0:SSM_H] = jax.nn.softplus(dtr_ref[...] + dtb_ref[...])
    a_row = -jnp.exp(alog_ref[...])
    r8 = _iota2((8, 8), 0)
    c8 = _iota2((8, 8), 1)
    mask8 = ((r8 >> 2) == (c8 >> 2)) & (r8 >= c8)

    def body(p, carry):
        rs = pl.ds(pl.multiple_of(p * 8, 8), 8)
        segs = [(0, 4, s0_ref.at[2 * p], sn_ref.at[2 * p]),
                (4, 8, s0_ref.at[2 * p + 1], sn_ref.at[2 * p + 1])]
        _ssd_chunk(act_scr, rs, dt_scr[rs, :], a_row, dx_ref, y_ref, mask8, segs, 8)
        return carry

    lax.fori_loop(0, rows // 8, body, 0)


def _ssd_sample(xbc, dtr, cs2d, s0, cw, cb, dtb, alog, dx, sel):
    T = xbc.shape[0]
    nb = s0.shape[0]
    rb = _SB * DEC_SEQ
    cbk = _SB * 3
    st = pl.BlockSpec((_SB, SSM_DI, SSM_N), lambda i: (i, 0, 0))
    return pl.pallas_call(
        _ssd_sample_kernel,
        out_shape=(jax.ShapeDtypeStruct((T, SSM_DI), f32), jax.ShapeDtypeStruct((nb, SSM_DI, SSM_N), f32)),
        grid=(T // rb,),
        in_specs=[pl.BlockSpec((rb, SSM_CONV_DIM), lambda i: (i, 0)), pl.BlockSpec((rb, SSM_H), lambda i: (i, 0)),
                  pl.BlockSpec((cbk, SSM_CONV_DIM), lambda i: (i, 0)), st,
                  _cspec((4, SSM_CONV_DIM)), _cspec((1, SSM_CONV_DIM)), _cspec((1, SSM_H)),
                  _cspec((1, 128)), _cspec((1, SSM_DI)), _cspec((3, rb, cbk))],
        out_specs=(pl.BlockSpec((rb, SSM_DI), lambda i: (i, 0)), st),
        scratch_shapes=[pltpu.VMEM((rb, SSM_CONV_DIM), f32), pltpu.VMEM((rb, 128), f32)],
        compiler_params=_params("arbitrary"),
        name="ssd_sample",
    )(xbc, dtr, cs2d, s0, cw, cb, dtb, alog, dx, sel)


def _ssd_post_kernel(y_ref, z_ref, x_ref, gn_ref, wout_ref, out_ref):
    out_ref[...] = _ssd_post_body(y_ref[...], z_ref[...], x_ref[...], gn_ref, wout_ref)


def _ssd_post(y, z, x, gn, wout, tm=512):
    T = x.shape[0]
    row = pl.BlockSpec((tm, D), lambda i: (i, 0))
    wide = pl.BlockSpec((tm, SSM_DI), lambda i: (i, 0))
    return pl.pallas_call(
        _ssd_post_kernel,
        out_shape=jax.ShapeDtypeStruct((T, D), f32),
        grid=(T // tm,),
        in_specs=[wide, wide, row, _cspec((1, SSM_DI)), _cspec((SSM_DI, D))],
        out_specs=row,
        compiler_params=_params("arbitrary"),
        name="ssd_post",
    )(y, z, x, gn.reshape(1, SSM_DI), wout)


def _pool_selectors():
    sel = np.zeros((4, _SB * DEC_SEQ, _SB * POOL_HIST), np.float32)
    for gi, w in enumerate(POOL_WINDOWS):
        for s in range(_SB):
            for t in range(DEC_SEQ):
                for i in range(POOL_HIST):
                    if i >= POOL_HIST + 1 + t - w:
                        sel[gi, s * DEC_SEQ + t, s * POOL_HIST + i] = 1.0
    return sel


def _conv_selectors():
    sel = np.zeros((3, _SB * DEC_SEQ, _SB * 3), np.float32)
    for j in range(3):
        for s in range(_SB):
            for t in range(DEC_SEQ):
                if t + j < 3:
                    sel[j, s * DEC_SEQ + t, s * 3 + t + j] = 1.0
    return sel


def kernel(x_prompt, x_sample, state_pool_l1, state_gla_l2, state_ssm_l3, state_conv_l3, p_prompt, p_sample, norm_ffn1, ffn1_gate, ffn1_up, ffn1_down, norm_mix, norm_ffn2, ffn2_gate, ffn2_up, ffn2_down, norm_ple, ple_gate, ple_proj, norm_final, gm_w_in, gm_ln, gm_w_s, gm_b_s, gm_w_out, pool_w, pool_scale, gla_w_in, gla_w_a1, gla_w_a2, gla_b_a, gla_norm, gla_w_out, ssm_w_in, ssm_conv_w, ssm_conv_b, ssm_dt_bias, ssm_a_log, ssm_d, ssm_norm, ssm_w_out):
    nbp, seq, _ = x_prompt.shape
    nbs, dseq, _ = x_sample.shape
    assert dseq == DEC_SEQ and seq % 512 == 0 and nbs % _GLA_SB == 0
    cast = lambda w: w.astype(bf16)

    w1g, w1u, w1d = cast(ffn1_gate), cast(ffn1_up), cast(ffn1_down)
    w2g, w2u, w2d = cast(ffn2_gate), cast(ffn2_up), cast(ffn2_down)
    wpg, wpp = cast(ple_gate), cast(ple_proj)
    gm_in, gm_out = cast(gm_w_in), cast(gm_w_out)
    pw = cast(pool_w)
    gla_in, gla_a1, gla_a2, gla_out = cast(gla_w_in), cast(gla_w_a1), cast(gla_w_a2), cast(gla_w_out)
    ssm_in, ssm_out = cast(ssm_w_in), cast(ssm_w_out)
    lane_pad = lambda v: jnp.pad(v, [(0, 0)] * (v.ndim - 1) + [(0, 128 - SSM_H)])
    ssm_wdt = lane_pad(cast(ssm_w_in[:, SSM_DI + SSM_CONV_DIM:]))
    n1g, n2g, npg = (n.reshape(DEPTH, 1, D) for n in (norm_ffn1, norm_ffn2, norm_ple))

    dx = jnp.repeat(ssm_d, SSM_P).reshape(1, SSM_DI)
    dtb = ssm_dt_bias.reshape(1, SSM_H)
    alog = ssm_a_log.reshape(1, SSM_H)
    cbias = ssm_conv_b.reshape(1, SSM_CONV_DIM)
    bst = jnp.transpose(gm_b_s)

    t8 = np.arange(8) % DEC_SEQ
    wj = []
    for j in range(DEC_SEQ):
        src = t8 - j
        vals = gm_w_s[:, t8, np.maximum(src, 0)]
        vals = jnp.where(jnp.asarray(src >= 0)[None, :], vals, 0.0)
        wj.append(jnp.repeat(jnp.transpose(vals), 128, axis=1))
    wj = jnp.stack(wj)
    bt = jnp.repeat(jnp.transpose(gm_b_s[:, t8]), 128, axis=1)

    pool_sel = jnp.asarray(_pool_selectors(), bf16)
    conv_sel = jnp.asarray(_conv_selectors(), bf16)

    xp = x_prompt.reshape(nbp * seq, D)
    xs = x_sample.reshape(nbs * dseq, D)
    pp = p_prompt.reshape(DEPTH, nbp * seq, D_PLE)
    ps = p_sample.reshape(DEPTH, nbs * dseq, D_PLE)
    hist2d = state_pool_l1.reshape(nbs * POOL_HIST, D)
    cs2d = state_conv_l3.reshape(nbs * 3, SSM_CONV_DIM)
    ssm_s0 = state_ssm_l3.reshape(nbs, SSM_DI, SSM_N)

    outs = {}
    for i in range(DEPTH):
        xp, xs = _ffn(xp, xs, i, n1g, w1g, w1u, w1d)
        gmix = norm_mix[i]
        if i == 0:
            xp = _gmlp_prompt(xp, gmix, gm_in, gm_ln, gm_w_s, bst, gm_out)
            xs, chunk_v = _gmlp_sample(xs, gmix, gm_in, gm_ln, wj, bt, gm_out)
            outs["chunk_v"] = chunk_v.reshape(nbs, dseq, D)
        elif i == 1:
            xp, outs["pool_p"] = _pool_prompt(xp, gmix, pw, pool_scale, nbp)
            xs, hs = _pool_sample(xs, hist2d, gmix, pw, pool_scale, pool_sel, PAST_LEN)
            outs["pool_s"] = jnp.concatenate([state_pool_l1[:, dseq:], hs.reshape(nbs, dseq, D)], axis=1)
        elif i == 2:
            xp, outs["gla_p"] = _gla_prompt(xp, gmix, gla_in, gla_a1, gla_a2, gla_b_a, gla_norm, gla_out, nbp)
            qkvr, la = _gla_pre(xs, gmix, gla_in, gla_a1, gla_a2, gla_b_a)
            o, outs["gla_s"] = _gla_sample(qkvr, la, state_gla_l2)
            xs = _gla_post(o, qkvr, xs, gla_norm, gla_out)
        else:
            xp, sn, outs["conv_p"] = _ssd_prompt(xp, gmix, ssm_in, ssm_wdt, ssm_conv_w, cbias, lane_pad(dtb),
                                                 lane_pad(alog), dx, ssm_norm, ssm_out, nbp)
            outs["ssm_p"] = sn.reshape(nbp, SSM_H, SSM_P, SSM_N)
            z, xbc, dtr = _ssd_pre(xs, gmix, ssm_in)
            y, sn = _ssd_sample(xbc, dtr, cs2d, ssm_s0, ssm_conv_w, cbias, dtb, lane_pad(alog), dx, conv_sel)
            outs["ssm_s"] = sn.reshape(nbs, SSM_H, SSM_P, SSM_N)
            outs["conv_s"] = xbc.reshape(nbs, dseq, SSM_CONV_DIM)[:, dseq - 3:]
            xs = _ssd_post(y, z, xs, ssm_norm, ssm_out)
        fin = norm_final if i == DEPTH - 1 else None
        xp, xs = _ffn(xp, xs, i, n2g, w2g, w2u, w2d, ple=(pp, ps, npg, wpg, wpp), final_g=fin)

    return (xp.reshape(nbp, seq, D), xs.reshape(nbs, dseq, D), outs["chunk_v"], outs["pool_p"], outs["pool_s"],
            outs["gla_p"], outs["gla_s"], outs["ssm_p"], outs["ssm_s"], outs["conv_p"], outs["conv_s"])
```

```python
import functools

import jax
import jax.numpy as jnp
import numpy as np
from jax import lax
from jax.experimental import pallas as pl
from jax.experimental.pallas import tpu as pltpu

f32 = jnp.float32
bf16 = jnp.bfloat16

D = 1024
F = 2816
FC = 256
NF = F // FC
D_PLE = 256
EPS = 1e-6
DEPTH = 4

GM_CHUNK = 128
GM_GROUPS = 8
POOL_WINDOWS = (2, 4, 8, 16)
POOL_GD = D // 4
POOL_HIST = 15

GLA_H = 4
GLA_DK = 128
GLA_DV = 256
GLA_QK = GLA_H * GLA_DK
GLA_V = GLA_H * GLA_DV
GLA_NORMALIZER = 16.0
GLA_C = 128

SSM_DI = 2048
SSM_P = 64
SSM_H = 32
SSM_G = 4
SSM_HPG = 8
SSM_N = 128
SSM_CONV_DIM = SSM_DI + 2 * SSM_G * SSM_N
SSM_C = 128
GD = SSM_HPG * SSM_P

DEC_SEQ = 4
PAST_LEN = 16384

VMEM_LIMIT = 60 * 1024 * 1024


def _dot(a, b):
    return jnp.dot(a, b, preferred_element_type=f32)


def _dot_nt(a, b):
    return lax.dot_general(a, b, (((1,), (1,)), ((), ())), preferred_element_type=f32)


def _dot_tn(a, b):
    return lax.dot_general(a, b, (((0,), (0,)), ((), ())), preferred_element_type=f32)


def _dot_hi(a, b):
    return jnp.dot(a, b, precision=lax.Precision.HIGHEST, preferred_element_type=f32)


def _split3(x):
    hi = x.astype(bf16)
    r1 = x - hi.astype(f32)
    mid = r1.astype(bf16)
    lo = (r1 - mid.astype(f32)).astype(bf16)
    return hi, mid, lo


def _sel_dot(sel, parts):
    hi, mid, lo = parts
    return _dot(sel, hi) + _dot(sel, mid) + _dot(sel, lo)


def _rms(x, g):
    return x * lax.rsqrt(jnp.mean(x * x, axis=-1, keepdims=True) + EPS) * g


def _sigmoid(x):
    return 0.5 + 0.5 * jnp.tanh(0.5 * x)


def _silu(x):
    hx = 0.5 * x
    return hx + hx * jnp.tanh(hx)


def _cspec(shape):
    nd = len(shape)
    return pl.BlockSpec(shape, lambda *_: (0,) * nd, pipeline_mode=pl.Buffered(1))


def _params(*sem):
    return pltpu.CompilerParams(dimension_semantics=tuple(sem), vmem_limit_bytes=VMEM_LIMIT)


def _iota2(shape, dim):
    return lax.broadcasted_iota(jnp.int32, shape, dim)


def _two_stage(step, bufs0, bufs1):
    par = lax.rem(pl.program_id(0), 2)

    @pl.when(par == 0)
    def _():
        step(bufs0, bufs1)

    @pl.when(par == 1)
    def _():
        step(bufs1, bufs0)


def _stage_specs(tm, nt):
    a = pl.BlockSpec((tm, D), lambda s: (jnp.minimum(s, nt - 1), 0))
    b = pl.BlockSpec((tm, D), lambda s: (jnp.clip(s - 1, 0, nt - 1), 0))
    return a, b


def _ffn_kernel(*refs, ple, final, n_prompt):
    xp_ref, xs_ref, g_ref, wg_ref, wu_ref, wd_ref = refs[:6]
    k = 6
    if ple:
        pp_ref, ps_ref, gp_ref, wpg_ref, wpp_ref = refs[k:k + 5]
        k += 5
    if final:
        gf_ref = refs[k]
        k += 1
    op_ref, os_ref, h_scr, acc_scr = refs[k:k + 4]

    is_prompt = pl.program_id(0) < n_prompt
    x = jnp.where(is_prompt, xp_ref[...], xs_ref[...])
    h_scr[...] = _rms(x, g_ref[...]).astype(bf16)
    for j in range(NF):
        sl = slice(j * FC, (j + 1) * FC)
        hb = h_scr[...]
        a = _silu(_dot(hb, wg_ref[:, sl].astype(bf16))) * _dot(hb, wu_ref[:, sl].astype(bf16))
        contrib = _dot(a.astype(bf16), wd_ref[sl, :].astype(bf16))
        if j == 0:
            acc_scr[...] = contrib
        else:
            acc_scr[...] += contrib
    y = x + 0.5 * acc_scr[...]
    if ple:
        h2 = _rms(y, gp_ref[...]).astype(bf16)
        gate = _sigmoid(_dot(h2, wpg_ref[...]))
        p = jnp.where(is_prompt, pp_ref[...], ps_ref[...]).astype(bf16)
        y = y + gate * _dot(p, wpp_ref[...])
    if final:
        y = _rms(y, gf_ref[...])

    @pl.when(is_prompt)
    def _():
        op_ref[...] = y

    @pl.when(jnp.logical_not(is_prompt))
    def _():
        os_ref[...] = y


def _lspec(shape, layer):
    nd = len(shape)
    return pl.BlockSpec((None,) + tuple(shape), lambda *_: (layer,) + (0,) * nd,
                        pipeline_mode=pl.Buffered(1))


def _ffn(xp, xs, layer, g, wg, wu, wd, ple=None, final_g=None):
    tm = xs.shape[0]
    n_prompt = xp.shape[0] // tm
    prow = lambda w: pl.BlockSpec((tm, w), lambda i: (jnp.minimum(i, n_prompt - 1), 0))
    srow = lambda w: pl.BlockSpec((tm, w), lambda i: (0, 0), pipeline_mode=pl.Buffered(1))
    in_specs = [prow(D), srow(D), _lspec((1, D), layer), _lspec((D, F), layer), _lspec((D, F), layer),
                _lspec((F, D), layer)]
    args = [xp, xs, g, wg, wu, wd]
    if ple is not None:
        pp, ps, gp, wpg, wpp = ple
        in_specs += [pl.BlockSpec((None, tm, D_PLE), lambda i: (layer, jnp.minimum(i, n_prompt - 1), 0)),
                     pl.BlockSpec((None, tm, D_PLE), lambda i: (layer, 0, 0), pipeline_mode=pl.Buffered(1)),
                     _lspec((1, D), layer), _lspec((D, D), layer), _lspec((D_PLE, D), layer)]
        args += [pp, ps, gp, wpg, wpp]
    if final_g is not None:
        in_specs.append(_cspec((1, D)))
        args.append(final_g.reshape(1, D))
    return pl.pallas_call(
        functools.partial(_ffn_kernel, ple=ple is not None, final=final_g is not None, n_prompt=n_prompt),
        out_shape=(jax.ShapeDtypeStruct(xp.shape, f32), jax.ShapeDtypeStruct(xs.shape, f32)),
        grid=(n_prompt + 1,),
        in_specs=in_specs,
        out_specs=(prow(D), pl.BlockSpec((tm, D), lambda i: (0, 0))),
        scratch_shapes=[pltpu.VMEM((tm, D), bf16), pltpu.VMEM((tm, D), f32)],
        compiler_params=_params("arbitrary"),
        name="ffn",
    )(*args)


def _gmlp_act(y, ln_ref):
    y = jax.nn.gelu(y, approximate=True)
    u = y[:, :D]
    v = y[:, D:]
    vc = v - jnp.mean(v, axis=-1, keepdims=True)
    v = vc * lax.rsqrt(jnp.mean(vc * vc, axis=-1, keepdims=True) + EPS) * ln_ref[...]
    return u, v


def _gmlp_pre(x, g, win_ref, ln_ref):
    h = _rms(x, g).astype(bf16)
    return _gmlp_act(_dot(h, win_ref[...]), ln_ref)


def _gmlp_prompt_kernel(x_ref, g_ref, win_ref, ln_ref, ws_ref, bst_ref, wout_ref, o_ref,
                        u_scr, vb_scr, t_scr):
    tm = x_ref.shape[0]
    x = x_ref[...]
    u, v = _gmlp_pre(x, g_ref[...], win_ref, ln_ref)
    u_scr[...] = u
    vb_scr[...] = v.astype(bf16)
    causal = _iota2((GM_CHUNK, GM_CHUNK), 0) >= _iota2((GM_CHUNK, GM_CHUNK), 1)
    for g in range(GM_GROUPS):
        wm = jnp.where(causal, ws_ref[g], 0.0).astype(bf16)
        bias = bst_ref[:, g:g + 1]
        cs = slice(g * 128, (g + 1) * 128)
        for ch in range(tm // GM_CHUNK):
            rs = slice(ch * GM_CHUNK, (ch + 1) * GM_CHUNK)
            sv = _dot(wm, vb_scr[rs, cs]) + bias
            t_scr[rs, cs] = (u_scr[rs, cs] * sv).astype(bf16)
    o_ref[...] = x + _dot(t_scr[...], wout_ref[...])


def _gmlp_prompt(x, g, win, ln, ws, bst, wout, tm=512):
    T = x.shape[0]
    row = pl.BlockSpec((tm, D), lambda i: (i, 0))
    return pl.pallas_call(
        _gmlp_prompt_kernel,
        out_shape=jax.ShapeDtypeStruct((T, D), f32),
        grid=(T // tm,),
        in_specs=[row, _cspec((1, D)), _cspec((D, 2 * D)), _cspec((1, D)),
                  _cspec((GM_GROUPS, GM_CHUNK, GM_CHUNK)), _cspec((GM_CHUNK, GM_GROUPS)),
                  _cspec((D, D))],
        out_specs=row,
        scratch_shapes=[pltpu.VMEM((tm, D), f32), pltpu.VMEM((tm, D), bf16), pltpu.VMEM((tm, D), bf16)],
        compiler_params=_params("arbitrary"),
        name="gmlp_prompt",
    )(x, g.reshape(1, D), win, ln.reshape(1, D), ws, bst, wout)


def _gmlp_sample_kernel(x_ref, g_ref, win_ref, ln_ref, wj_ref, bt_ref, wout_ref, o_ref, v_ref):
    rows = x_ref.shape[0]
    n = rows // 8
    x = x_ref[...]
    u, v = _gmlp_pre(x, g_ref[...], win_ref, ln_ref)
    v_ref[...] = v
    acc = v.reshape(n, 8, D) * wj_ref[0][None]
    for j in range(1, DEC_SEQ):
        acc = acc + pltpu.roll(v, j, axis=0).reshape(n, 8, D) * wj_ref[j][None]
    sv = (acc + bt_ref[...][None]).reshape(rows, D)
    o_ref[...] = x + _dot((u * sv).astype(bf16), wout_ref[...])


def _gmlp_sample(x, g, win, ln, wj, bt, wout):
    T = x.shape[0]
    full = pl.BlockSpec((T, D), lambda i: (0, 0))
    return pl.pallas_call(
        _gmlp_sample_kernel,
        out_shape=(jax.ShapeDtypeStruct((T, D), f32), jax.ShapeDtypeStruct((T, D), f32)),
        grid=(1,),
        in_specs=[full, _cspec((1, D)), _cspec((D, 2 * D)), _cspec((1, D)),
                  _cspec((DEC_SEQ, 8, D)), _cspec((8, D)), _cspec((D, D))],
        out_specs=(full, full),
        compiler_params=_params("arbitrary"),
        name="gmlp_sample",
    )(x, g.reshape(1, D), win, ln.reshape(1, D), wj, bt, wout)


def _pool_prompt_kernel(x_ref, g_ref, pw_ref, sc_ref, o_ref, pn_ref, ext_scr, *, nl):
    tm = x_ref.shape[0]
    l = pl.program_id(1)

    @pl.when(l == 0)
    def _():
        ext_scr[0:16, :] = jnp.zeros((16, D), f32)

    x = x_ref[...]
    ext_scr[16:16 + tm, :] = _rms(x, g_ref[...])
    pos = l * tm + _iota2((tm, 1), 0)
    for gi, w in enumerate(POOL_WINDOWS):
        cs = slice(gi * POOL_GD, (gi + 1) * POOL_GD)
        h = ext_scr[16:16 + tm, cs]
        s = h
        for j in range(1, w):
            s = s + ext_scr[16 - j:16 - j + tm, cs]
        cnt = jnp.minimum(pos + 1, w).astype(f32)
        diff = (s / cnt - h).astype(bf16)
        o_ref[:, cs] = x[:, cs] + _dot(diff, pw_ref[gi]) * sc_ref[:, cs]

    @pl.when(l == nl - 1)
    def _():
        pn_ref[0] = ext_scr[16 + tm - POOL_HIST:16 + tm, :]

    ext_scr[0:16, :] = ext_scr[tm:tm + 16, :]


def _pool_prompt(x, g, pw, sc, nb, tm=512):
    T = x.shape[0]
    nl = T // nb // tm
    row = pl.BlockSpec((tm, D), lambda b, l: (b * nl + l, 0))
    return pl.pallas_call(
        functools.partial(_pool_prompt_kernel, nl=nl),
        out_shape=(jax.ShapeDtypeStruct((T, D), f32), jax.ShapeDtypeStruct((nb, POOL_HIST, D), f32)),
        grid=(nb, nl),
        in_specs=[row, _cspec((1, D)), _cspec((4, POOL_GD, POOL_GD)), _cspec((1, D))],
        out_specs=(row, pl.BlockSpec((1, POOL_HIST, D), lambda b, l: (b, 0, 0))),
        scratch_shapes=[pltpu.VMEM((tm + 16, D), f32)],
        compiler_params=_params("arbitrary", "arbitrary"),
        name="pool_prompt",
    )(x, g.reshape(1, D), pw, sc.reshape(1, D))


_SB = 8


def _pool_sample_kernel(x_ref, hist_ref, g_ref, pw_ref, sc_ref, sel_ref, o_ref, h_ref, *, pos0):
    rows = x_ref.shape[0]
    x = x_ref[...]
    h = _rms(x, g_ref[...])
    h_ref[...] = h
    t = _iota2((rows, 1), 0) & (DEC_SEQ - 1)
    parts = _split3(hist_ref[...])
    for gi, w in enumerate(POOL_WINDOWS):
        cs = slice(gi * POOL_GD, (gi + 1) * POOL_GD)
        hc = h[:, cs]
        s = hc
        for j in range(1, min(w, DEC_SEQ)):
            s = s + jnp.where(t >= j, pltpu.roll(hc, j, axis=0), 0.0)
        s = s + _sel_dot(sel_ref[gi], tuple(p[:, cs] for p in parts))
        cnt = jnp.minimum(pos0 + t + 1, w).astype(f32)
        diff = (s / cnt - hc).astype(bf16)
        o_ref[:, cs] = x[:, cs] + _dot(diff, pw_ref[gi]) * sc_ref[:, cs]


def _pool_sample(x, hist2d, g, pw, sc, sel, pos0):
    T = x.shape[0]
    rb = _SB * DEC_SEQ
    hb = _SB * POOL_HIST
    row = pl.BlockSpec((rb, D), lambda i: (i, 0))
    return pl.pallas_call(
        functools.partial(_pool_sample_kernel, pos0=pos0),
        out_shape=(jax.ShapeDtypeStruct((T, D), f32), jax.ShapeDtypeStruct((T, D), f32)),
        grid=(T // rb,),
        in_specs=[row, pl.BlockSpec((hb, D), lambda i: (i, 0)), _cspec((1, D)),
                  _cspec((4, POOL_GD, POOL_GD)), _cspec((1, D)), _cspec((4, rb, hb))],
        out_specs=(row, row),
        compiler_params=_params("arbitrary"),
        name="pool_sample",
    )(x, hist2d, g.reshape(1, D), pw, sc.reshape(1, D), sel)


def _cumsum_rows(x, seg=None):
    n = x.shape[0]
    seg = n if seg is None else seg
    pos = _iota2((n, 1), 0) & (seg - 1)
    s = 1
    while s < seg:
        x = x + jnp.where(pos >= s, pltpu.roll(x, s, axis=0), 0.0)
        s *= 2
    return x


def _gla_pre_body(x, g_ref, win_ref, wa1_ref, wa2_ref, ba_ref, qkvr_ref, la_ref):
    h = _rms(x, g_ref[...]).astype(bf16)
    qkvr_ref[...] = _dot(h, win_ref[...])
    a1 = _dot(h, wa1_ref[...]).astype(bf16)
    z = _dot(a1, wa2_ref[...]) + ba_ref[...]
    la_ref[...] = jax.nn.log_sigmoid(z) / GLA_NORMALIZER


def _gla_post_body(o, r, x, gn_ref, wout_ref):
    parts = []
    for hd in range(GLA_H):
        vs = slice(hd * GLA_DV, (hd + 1) * GLA_DV)
        parts.append(_rms(o[:, vs], gn_ref[:, vs]))
    on = jnp.concatenate(parts, axis=1)
    gated = (on * _silu(r)).astype(bf16)
    return x + _dot(gated, wout_ref[...])


def _gla_pre_kernel(x_ref, g_ref, win_ref, wa1_ref, wa2_ref, ba_ref, qkvr_ref, la_ref):
    _gla_pre_body(x_ref[...], g_ref, win_ref, wa1_ref, wa2_ref, ba_ref, qkvr_ref, la_ref)


def _gla_pre(x, g, win, wa1, wa2, ba, tm=512):
    T = x.shape[0]
    W = 2 * GLA_QK + 2 * GLA_V
    return pl.pallas_call(
        _gla_pre_kernel,
        out_shape=(jax.ShapeDtypeStruct((T, W), f32), jax.ShapeDtypeStruct((T, GLA_QK), f32)),
        grid=(T // tm,),
        in_specs=[pl.BlockSpec((tm, D), lambda i: (i, 0)), _cspec((1, D)), _cspec((D, W)),
                  _cspec((D, 16)), _cspec((16, GLA_QK)), _cspec((1, GLA_QK))],
        out_specs=(pl.BlockSpec((tm, W), lambda i: (i, 0)), pl.BlockSpec((tm, GLA_QK), lambda i: (i, 0))),
        compiler_params=_params("arbitrary"),
        name="gla_pre",
    )(x, g.reshape(1, D), win, wa1, wa2, ba.reshape(1, GLA_QK))


def _gla_chunk(q, k, v, b, bl, anc, mask, segs, C, halves=False):
    qs = q * (GLA_DK ** -0.5)
    q_dec = (qs * jnp.exp(b)).astype(bf16)
    k_end = k * jnp.exp(bl - b)
    q_mid = (qs * jnp.exp(b - anc)).astype(bf16)
    k_mid = (k * jnp.exp(anc - b)).astype(bf16)
    ebl = jnp.exp(bl)
    vb = v.astype(bf16)
    rid = _iota2((C, 1), 0)
    if halves:
        hc = C // 2
        edge = b[hc - 1:hc, :]
        q_x = jnp.where(rid >= hc, qs * jnp.exp(b - edge), 0.0).astype(bf16)
        k_x = jnp.where(rid < hc, k * jnp.exp(edge - b), 0.0).astype(bf16)
        mask = mask & ((_iota2((C, C), 0) >= hc) == (_iota2((C, C), 1) >= hc))
    outs = []
    for hd in range(GLA_H):
        ds_ = slice(hd * GLA_DK, (hd + 1) * GLA_DK)
        vs = slice(hd * GLA_DV, (hd + 1) * GLA_DV)
        sc = jnp.where(mask, _dot_nt(q_mid[:, ds_], k_mid[:, ds_]), 0.0)
        if halves:
            sc = sc + _dot_nt(q_x[:, ds_], k_x[:, ds_])
        o = _dot(sc.astype(bf16), vb[:, vs])
        o_int = None
        for (lo, hi, s_in, s_out) in segs:
            oi = _dot(q_dec[:, ds_], s_in[hd].astype(bf16))
            o_int = oi if o_int is None else jnp.where(rid >= lo, oi, o_int)
        outs.append(o + o_int)
        for (lo, hi, s_in, s_out) in segs:
            if ebl.shape[0] == 1:
                dec = jnp.transpose(jnp.broadcast_to(ebl[:, ds_], (8, GLA_DK)))[:, 0:1]
            else:
                r0 = max(hi - 8, 0)
                dec = jnp.transpose(ebl[r0:r0 + 8, ds_])[:, hi - 1 - r0:hi - r0]
            ke = k_end[:, ds_]
            if len(segs) > 1:
                ke = jnp.where((rid >= lo) & (rid < hi), ke, 0.0)
            s_out[hd] = s_in[hd] * dec + _dot_tn(ke.astype(bf16), vb[:, vs])
    return jnp.concatenate(outs, axis=1)


def _gla_prompt_kernel(xa_ref, xb_ref, g_ref, win_ref, wa1_ref, wa2_ref, ba_ref, gn_ref, wout_ref,
                       out_ref, sn_ref, q0, l0, q1, l1, o_scr, s_scr, *, nl):
    tm = xa_ref.shape[0]
    C = GLA_C
    s = pl.program_id(0)
    tpos = lax.rem(s - 1, nl)

    @pl.when(s == 0)
    def _():
        q1[...] = jnp.zeros_like(q1)
        l1[...] = jnp.zeros_like(l1)

    @pl.when((s == 0) | (tpos == 0))
    def _():
        s_scr[...] = jnp.zeros_like(s_scr)

    def step(bw, br):
        qw, lw = bw
        qr, lr = br
        _gla_pre_body(xa_ref[...], g_ref, win_ref, wa1_ref, wa2_ref, ba_ref, qw, lw)
        tri = _iota2((C, C), 0) >= _iota2((C, C), 1)
        for c in range(tm // C):
            rs = slice(c * C, (c + 1) * C)
            b = _cumsum_rows(lr[rs, :])
            bl = b[C - 1:C, :]
            anc = jnp.where(_iota2((C, 1), 0) < C // 2, b[C // 4 - 1:C // 4, :], b[3 * C // 4 - 1:3 * C // 4, :])
            q = qr[rs, 0:GLA_QK]
            k = qr[rs, GLA_QK:2 * GLA_QK]
            v = qr[rs, 2 * GLA_QK:2 * GLA_QK + GLA_V]
            o_scr[rs, :] = _gla_chunk(q, k, v, b, bl, anc, tri, [(0, C, s_scr, s_scr)], C, halves=True)
        out_ref[...] = _gla_post_body(o_scr[...], qr[:, 2 * GLA_QK + GLA_V:], xb_ref[...], gn_ref, wout_ref)

    _two_stage(step, (q0, l0), (q1, l1))

    @pl.when(tpos == nl - 1)
    def _():
        sn_ref[0] = s_scr[...]


def _gla_prompt(x, g, win, wa1, wa2, ba, gn, wout, nb, tm=256):
    T = x.shape[0]
    nt = T // tm
    nl = nt // nb
    W = 2 * GLA_QK + 2 * GLA_V
    a, b = _stage_specs(tm, nt)
    return pl.pallas_call(
        functools.partial(_gla_prompt_kernel, nl=nl),
        out_shape=(jax.ShapeDtypeStruct((T, D), f32),
                   jax.ShapeDtypeStruct((nb, GLA_H, GLA_DK, GLA_DV), f32)),
        grid=(nt + 1,),
        in_specs=[a, b, _cspec((1, D)), _cspec((D, W)), _cspec((D, 16)), _cspec((16, GLA_QK)),
                  _cspec((1, GLA_QK)), _cspec((1, GLA_V)), _cspec((GLA_V, D))],
        out_specs=(b, pl.BlockSpec((1, GLA_H, GLA_DK, GLA_DV), lambda s: (jnp.maximum(s - 1, 0) // nl, 0, 0, 0))),
        scratch_shapes=[pltpu.VMEM((tm, W), f32), pltpu.VMEM((tm, GLA_QK), f32),
                        pltpu.VMEM((tm, W), f32), pltpu.VMEM((tm, GLA_QK), f32),
                        pltpu.VMEM((tm, GLA_V), f32), pltpu.VMEM((GLA_H, GLA_DK, GLA_DV), f32)],
        compiler_params=_params("arbitrary"),
        name="gla_prompt",
    )(x, x, g.reshape(1, D), win, wa1, wa2, ba.reshape(1, GLA_QK), gn.reshape(1, GLA_V), wout)


_GLA_SB = 16


def _gla_sample_kernel(qkv_ref, la_ref, s0_ref, o_ref, sn_ref, b_scr, bl_scr):
    rows = qkv_ref.shape[0]
    r = _iota2((rows, rows), 0)
    c = _iota2((rows, rows), 1)
    same = (r >> 2) == (c >> 2)
    la = la_ref[...]
    b_scr[...] = _dot_hi((same & (r >= c)).astype(f32), la)
    bl_scr[...] = _dot_hi(same.astype(f32), la)
    r8 = _iota2((8, 8), 0)
    c8 = _iota2((8, 8), 1)
    mask8 = ((r8 >> 2) == (c8 >> 2)) & (r8 >= c8)

    def body(p, carry):
        rs = pl.ds(pl.multiple_of(p * 8, 8), 8)
        q = qkv_ref[rs, 0:GLA_QK]
        k = qkv_ref[rs, GLA_QK:2 * GLA_QK]
        v = qkv_ref[rs, 2 * GLA_QK:2 * GLA_QK + GLA_V]
        segs = [(0, 4, s0_ref.at[2 * p], sn_ref.at[2 * p]),
                (4, 8, s0_ref.at[2 * p + 1], sn_ref.at[2 * p + 1])]
        o_ref[rs, :] = _gla_chunk(q, k, v, b_scr[rs, :], bl_scr[rs, :], 0.0, mask8, segs, 8)
        return carry

    lax.fori_loop(0, rows // 8, body, 0)


def _gla_sample(qkvr, la, s0):
    T = qkvr.shape[0]
    nb = s0.shape[0]
    rb = _GLA_SB * DEC_SEQ
    W = 2 * GLA_QK + GLA_V
    st = pl.BlockSpec((_GLA_SB, GLA_H, GLA_DK, GLA_DV), lambda i: (i, 0, 0, 0))
    return pl.pallas_call(
        _gla_sample_kernel,
        out_shape=(jax.ShapeDtypeStruct((T, GLA_V), f32),
                   jax.ShapeDtypeStruct((nb, GLA_H, GLA_DK, GLA_DV), f32)),
        grid=(T // rb,),
        in_specs=[pl.BlockSpec((rb, W), lambda i: (i, 0)), pl.BlockSpec((rb, GLA_QK), lambda i: (i, 0)), st],
        out_specs=(pl.BlockSpec((rb, GLA_V), lambda i: (i, 0)), st),
        scratch_shapes=[pltpu.VMEM((rb, GLA_QK), f32), pltpu.VMEM((rb, GLA_QK), f32)],
        compiler_params=_params("arbitrary"),
        name="gla_sample",
    )(qkvr, la, s0)


def _gla_post_kernel(o_ref, r_ref, x_ref, gn_ref, wout_ref, out_ref):
    out_ref[...] = _gla_post_body(o_ref[...], r_ref[...], x_ref[...], gn_ref, wout_ref)


def _gla_post(o, qkvr, x, gn, wout, tm=512):
    T = x.shape[0]
    row = pl.BlockSpec((tm, D), lambda i: (i, 0))
    return pl.pallas_call(
        _gla_post_kernel,
        out_shape=jax.ShapeDtypeStruct((T, D), f32),
        grid=(T // tm,),
        in_specs=[row, pl.BlockSpec((tm, GLA_V), lambda i: (i, 2)), row, _cspec((1, GLA_V)),
                  _cspec((GLA_V, D))],
        out_specs=row,
        compiler_params=_params("arbitrary"),
        name="gla_post",
    )(o, qkvr, x, gn.reshape(1, GLA_V), wout)


def _ssd_pre_kernel(x_ref, g_ref, win_ref, z_ref, xbc_ref, dt_ref):
    h = _rms(x_ref[...], g_ref[...]).astype(bf16)
    z_ref[...] = _dot(h, win_ref[:, 0:SSM_DI])
    xbc_ref[...] = _dot(h, win_ref[:, SSM_DI:SSM_DI + SSM_CONV_DIM])
    dt_ref[...] = _dot(h, win_ref[:, SSM_DI + SSM_CONV_DIM:])


def _ssd_pre(x, g, win, tm=512):
    T = x.shape[0]
    return pl.pallas_call(
        _ssd_pre_kernel,
        out_shape=(jax.ShapeDtypeStruct((T, SSM_DI), f32), jax.ShapeDtypeStruct((T, SSM_CONV_DIM), f32),
                   jax.ShapeDtypeStruct((T, SSM_H), f32)),
        grid=(T // tm,),
        in_specs=[pl.BlockSpec((tm, D), lambda i: (i, 0)), _cspec((1, D)), _cspec((D, win.shape[1]))],
        out_specs=(pl.BlockSpec((tm, SSM_DI), lambda i: (i, 0)),
                   pl.BlockSpec((tm, SSM_CONV_DIM), lambda i: (i, 0)),
                   pl.BlockSpec((tm, SSM_H), lambda i: (i, 0))),
        compiler_params=_params("arbitrary"),
        name="ssd_pre",
    )(x, g.reshape(1, D), win)


def _conv_silu(terms, cw_ref, cb_ref):
    acc = terms[0] * cw_ref[0:1, :]
    for j in range(1, 4):
        acc = acc + terms[j] * cw_ref[j:j + 1, :]
    return _silu(cb_ref[...] + acc)


def _ssd_chunk(act_ref, rs, dtc, a_row, dx_ref, y_ref, mask, segs, C):
    rid = _iota2((C, 1), 0)
    cum = _cumsum_rows(dtc * a_row, segs[0][1] - segs[0][0])
    cum_last = cum[segs[-1][1] - 1:segs[-1][1], :]
    for (lo, hi, _, _) in segs[-2::-1]:
        cum_last = jnp.where(rid < hi, cum[hi - 1:hi, :], cum_last)
    e_cum = jnp.exp(cum)
    wend = jnp.exp(cum_last - cum) * dtc
    cum_t = jnp.transpose(cum)
    dt_t = jnp.transpose(dtc)
    dec_t = jnp.exp(cum_t)
    lane_lo = _iota2((C, 2 * SSM_P), 1) < SSM_P
    for g in range(SSM_G):
        gs = slice(g * GD, (g + 1) * GD)
        bm = act_ref[rs, SSM_DI + g * SSM_N:SSM_DI + (g + 1) * SSM_N].astype(bf16)
        cm = act_ref[rs, SSM_DI + SSM_G * SSM_N + g * SSM_N:SSM_DI + SSM_G * SSM_N + (g + 1) * SSM_N].astype(bf16)
        cb = _dot_nt(cm, bm)
        y_int = None
        for (lo, hi, s_in, s_out) in segs:
            yi = _dot_nt(cm, s_in[gs, :].astype(bf16))
            y_int = yi if y_int is None else jnp.where(rid >= lo, yi, y_int)
        xw = []
        for pr in range(SSM_HPG // 2):
            h0 = g * SSM_HPG + 2 * pr
            cols = slice(g * GD + 2 * pr * SSM_P, g * GD + (2 * pr + 2) * SSM_P)
            xs = act_ref[rs, cols]
            e_pair = jnp.where(lane_lo, e_cum[:, h0:h0 + 1], e_cum[:, h0 + 1:h0 + 2])
            w_pair = jnp.where(lane_lo, wend[:, h0:h0 + 1], wend[:, h0 + 1:h0 + 2])
            acc = y_int[:, 2 * pr * SSM_P:(2 * pr + 2) * SSM_P] * e_pair + xs * dx_ref[:, cols]
            xp = xs.astype(bf16)
            for hh, keep in ((h0, lane_lo), (h0 + 1, ~lane_lo)):
                seg = cum[:, hh:hh + 1] - cum_t[hh:hh + 1, :]
                mix = cb * jnp.exp(jnp.where(mask, seg, -jnp.inf)) * dt_t[hh:hh + 1, :]
                acc = acc + _dot(mix.astype(bf16), jnp.where(keep, xp, jnp.zeros_like(xp)))
            y_ref[rs, cols] = acc
            xw.append(xs * w_pair)
        xw = jnp.concatenate(xw, axis=1)
        for (lo, hi, s_in, s_out) in segs:
            xws = xw if len(segs) == 1 else jnp.where((rid >= lo) & (rid < hi), xw, 0.0)
            upd = _dot_tn(xws.astype(bf16), bm)
            for r in range(SSM_HPG):
                hh = g * SSM_HPG + r
                hs = slice(hh * SSM_P, (hh + 1) * SSM_P)
                s_out[hs, :] = s_in[hs, :] * dec_t[hh:hh + 1, hi - 1:hi] + upd[r * SSM_P:(r + 1) * SSM_P, :]


def _ssd_post_body(y, z, x, gn_ref, wout_ref):
    y = y * _silu(z)
    parts = []
    for g in range(SSM_G):
        gs = slice(g * GD, (g + 1) * GD)
        parts.append(_rms(y[:, gs], gn_ref[:, gs]))
    yn = jnp.concatenate(parts, axis=1).astype(bf16)
    return x + _dot(yn, wout_ref[...])


def _ssd_prompt_kernel(xa_ref, xb_ref, g_ref, win_ref, wdt_ref, cw_ref, cb_ref, dtb_ref, alog_ref, dx_ref, gn_ref,
                       wout_ref, out_ref, sn_ref, cn_ref, z0, e0, d0, z1, e1, d1, carry_scr, act_scr, y_scr, s_scr,
                       *, nl):
    tm = xa_ref.shape[0]
    C = SSM_C
    s = pl.program_id(0)
    tpos = lax.rem(s - 1, nl)

    @pl.when(s == 0)
    def _():
        z1[...] = jnp.zeros_like(z1)
        e1[...] = jnp.zeros_like(e1)
        d1[...] = jnp.zeros_like(d1)

    @pl.when((s == 0) | (tpos == 0))
    def _():
        carry_scr[...] = jnp.zeros_like(carry_scr)
        s_scr[...] = jnp.zeros_like(s_scr)

    def step(bw, br):
        zw, ew, dw = bw
        zr, er, dr = br
        h = _rms(xa_ref[...], g_ref[...]).astype(bf16)
        zw[...] = _dot(h, win_ref[:, 0:SSM_DI])
        ew[8:8 + tm, :] = _dot(h, win_ref[:, SSM_DI:SSM_DI + SSM_CONV_DIM])
        dw[...] = _dot(h, wdt_ref[...])
        er[0:8, :] = carry_scr[...]
        e = er[...]
        acc = e * cw_ref[0:1, :]
        for j in range(1, 4):
            acc = pltpu.roll(acc, 1, axis=0) + e * cw_ref[j:j + 1, :]
        act_scr[...] = _silu(cb_ref[...] + acc[8:8 + tm, :])
        carry_scr[...] = er[tm:tm + 8, :]
        a_row = -jnp.exp(alog_ref[...])
        tri = _iota2((C, C), 0) >= _iota2((C, C), 1)
        for c in range(tm // C):
            rs = slice(c * C, (c + 1) * C)
            dtc = jax.nn.softplus(dr[rs, :] + dtb_ref[...])
            _ssd_chunk(act_scr, rs, dtc, a_row, dx_ref, y_scr, tri, [(0, C, s_scr, s_scr)], C)
        out_ref[...] = _ssd_post_body(y_scr[...], zr[...], xb_ref[...], gn_ref, wout_ref)

        @pl.when(tpos == nl - 1)
        def _():
            cn_ref[0] = er[8 + tm - 3:8 + tm, :]

    _two_stage(step, (z0, e0, d0), (z1, e1, d1))

    @pl.when(tpos == nl - 1)
    def _():
        sn_ref[0] = s_scr[...]


def _ssd_prompt(x, g, win, wdt, cw, cb, dtb, alog, dx, gn, wout, nb, tm=256):
    T = x.shape[0]
    nt = T // tm
    nl = nt // nb
    WIN = win.shape[1]
    a, b = _stage_specs(tm, nt)
    seq = lambda s: (jnp.maximum(s - 1, 0) // nl, 0, 0)
    bufs = [pltpu.VMEM((tm, SSM_DI), f32), pltpu.VMEM((tm + 8, SSM_CONV_DIM), f32), pltpu.VMEM((tm, 128), f32)]
    return pl.pallas_call(
        functools.partial(_ssd_prompt_kernel, nl=nl),
        out_shape=(jax.ShapeDtypeStruct((T, D), f32),
                   jax.ShapeDtypeStruct((nb, SSM_DI, SSM_N), f32),
                   jax.ShapeDtypeStruct((nb, 3, SSM_CONV_DIM), f32)),
        grid=(nt + 1,),
        in_specs=[a, b, _cspec((1, D)), _cspec((D, WIN)), _cspec((D, 128)),
                  _cspec((4, SSM_CONV_DIM)), _cspec((1, SSM_CONV_DIM)), _cspec((1, 128)),
                  _cspec((1, 128)), _cspec((1, SSM_DI)), _cspec((1, SSM_DI)), _cspec((SSM_DI, D))],
        out_specs=(b, pl.BlockSpec((1, SSM_DI, SSM_N), seq), pl.BlockSpec((1, 3, SSM_CONV_DIM), seq)),
        scratch_shapes=bufs + bufs + [pltpu.VMEM((8, SSM_CONV_DIM), f32), pltpu.VMEM((tm, SSM_CONV_DIM), f32),
                                      pltpu.VMEM((tm, SSM_DI), f32), pltpu.VMEM((SSM_DI, SSM_N), f32)],
        compiler_params=_params("arbitrary"),
        name="ssd_prompt",
    )(x, x, g.reshape(1, D), win, wdt, cw, cb, dtb, alog, dx, gn.reshape(1, SSM_DI), wout)


def _ssd_sample_kernel(xbc_ref, dtr_ref, cs_ref, s0_ref, cw_ref, cb_ref, dtb_ref, alog_ref, dx_ref,
                       sel_ref, y_ref, sn_ref, act_scr, dt_scr):
    rows = xbc_ref.shape[0]
    xb = xbc_ref[...]
    t = _iota2((rows, 1), 0) & (DEC_SEQ - 1)
    parts = _split3(cs_ref[...])
    terms = []
    for j in range(4):
        sh = 3 - j
        term = xb if sh == 0 else jnp.where(t >= sh, pltpu.roll(xb, sh, axis=0), 0.0)
        if j < 3:
            term = term + _sel_dot(sel_ref[j], parts)
        terms.append(term)
    act_scr[...] = _conv_silu(terms, cw_ref, cb_ref)
    dt_scr[...] = jnp.zeros_like(dt_scr)
    dt_scr[:, 0:SSM_H] = jax.nn.softplus(dtr_ref[...] + dtb_ref[...])
    a_row = -jnp.exp(alog_ref[...])
    r8 = _iota2((8, 8), 0)
    c8 = _iota2((8, 8), 1)
    mask8 = ((r8 >> 2) == (c8 >> 2)) & (r8 >= c8)

    def body(p, carry):
        rs = pl.ds(pl.multiple_of(p * 8, 8), 8)
        segs = [(0, 4, s0_ref.at[2 * p], sn_ref.at[2 * p]),
                (4, 8, s0_ref.at[2 * p + 1], sn_ref.at[2 * p + 1])]
        _ssd_chunk(act_scr, rs, dt_scr[rs, :], a_row, dx_ref, y_ref, mask8, segs, 8)
        return carry

    lax.fori_loop(0, rows // 8, body, 0)


def _ssd_sample(xbc, dtr, cs2d, s0, cw, cb, dtb, alog, dx, sel):
    T = xbc.shape[0]
    nb = s0.shape[0]
    rb = _SB * DEC_SEQ
    cbk = _SB * 3
    st = pl.BlockSpec((_SB, SSM_DI, SSM_N), lambda i: (i, 0, 0))
    return pl.pallas_call(
        _ssd_sample_kernel,
        out_shape=(jax.ShapeDtypeStruct((T, SSM_DI), f32), jax.ShapeDtypeStruct((nb, SSM_DI, SSM_N), f32)),
        grid=(T // rb,),
        in_specs=[pl.BlockSpec((rb, SSM_CONV_DIM), lambda i: (i, 0)), pl.BlockSpec((rb, SSM_H), lambda i: (i, 0)),
                  pl.BlockSpec((cbk, SSM_CONV_DIM), lambda i: (i, 0)), st,
                  _cspec((4, SSM_CONV_DIM)), _cspec((1, SSM_CONV_DIM)), _cspec((1, SSM_H)),
                  _cspec((1, 128)), _cspec((1, SSM_DI)), _cspec((3, rb, cbk))],
        out_specs=(pl.BlockSpec((rb, SSM_DI), lambda i: (i, 0)), st),
        scratch_shapes=[pltpu.VMEM((rb, SSM_CONV_DIM), f32), pltpu.VMEM((rb, 128), f32)],
        compiler_params=_params("arbitrary"),
        name="ssd_sample",
    )(xbc, dtr, cs2d, s0, cw, cb, dtb, alog, dx, sel)


def _ssd_post_kernel(y_ref, z_ref, x_ref, gn_ref, wout_ref, out_ref):
    out_ref[...] = _ssd_post_body(y_ref[...], z_ref[...], x_ref[...], gn_ref, wout_ref)


def _ssd_post(y, z, x, gn, wout, tm=512):
    T = x.shape[0]
    row = pl.BlockSpec((tm, D), lambda i: (i, 0))
    wide = pl.BlockSpec((tm, SSM_DI), lambda i: (i, 0))
    return pl.pallas_call(
        _ssd_post_kernel,
        out_shape=jax.ShapeDtypeStruct((T, D), f32),
        grid=(T // tm,),
        in_specs=[wide, wide, row, _cspec((1, SSM_DI)), _cspec((SSM_DI, D))],
        out_specs=row,
        compiler_params=_params("arbitrary"),
        name="ssd_post",
    )(y, z, x, gn.reshape(1, SSM_DI), wout)


def _pool_selectors():
    sel = np.zeros((4, _SB * DEC_SEQ, _SB * POOL_HIST), np.float32)
    for gi, w in enumerate(POOL_WINDOWS):
        for s in range(_SB):
            for t in range(DEC_SEQ):
                for i in range(POOL_HIST):
                    if i >= POOL_HIST + 1 + t - w:
                        sel[gi, s * DEC_SEQ + t, s * POOL_HIST + i] = 1.0
    return sel


def _conv_selectors():
    sel = np.zeros((3, _SB * DEC_SEQ, _SB * 3), np.float32)
    for j in range(3):
        for s in range(_SB):
            for t in range(DEC_SEQ):
                if t + j < 3:
                    sel[j, s * DEC_SEQ + t, s * 3 + t + j] = 1.0
    return sel


def kernel(x_prompt, x_sample, state_pool_l1, state_gla_l2, state_ssm_l3, state_conv_l3, p_prompt, p_sample, norm_ffn1, ffn1_gate, ffn1_up, ffn1_down, norm_mix, norm_ffn2, ffn2_gate, ffn2_up, ffn2_down, norm_ple, ple_gate, ple_proj, norm_final, gm_w_in, gm_ln, gm_w_s, gm_b_s, gm_w_out, pool_w, pool_scale, gla_w_in, gla_w_a1, gla_w_a2, gla_b_a, gla_norm, gla_w_out, ssm_w_in, ssm_conv_w, ssm_conv_b, ssm_dt_bias, ssm_a_log, ssm_d, ssm_norm, ssm_w_out):
    nbp, seq, _ = x_prompt.shape
    nbs, dseq, _ = x_sample.shape
    assert dseq == DEC_SEQ and seq % 512 == 0 and nbs % _GLA_SB == 0
    cast = lambda w: w.astype(bf16)

    w1g, w1u, w1d = ffn1_gate, ffn1_up, ffn1_down
    w2g, w2u, w2d = ffn2_gate, ffn2_up, ffn2_down
    wpg, wpp = cast(ple_gate), cast(ple_proj)
    gm_in, gm_out = cast(gm_w_in), cast(gm_w_out)
    pw = cast(pool_w)
    gla_in, gla_a1, gla_a2, gla_out = cast(gla_w_in), cast(gla_w_a1), cast(gla_w_a2), cast(gla_w_out)
    ssm_in, ssm_out = cast(ssm_w_in), cast(ssm_w_out)
    lane_pad = lambda v: jnp.pad(v, [(0, 0)] * (v.ndim - 1) + [(0, 128 - SSM_H)])
    ssm_wdt = lane_pad(cast(ssm_w_in[:, SSM_DI + SSM_CONV_DIM:]))
    n1g, n2g, npg = (n.reshape(DEPTH, 1, D) for n in (norm_ffn1, norm_ffn2, norm_ple))

    dx = jnp.repeat(ssm_d, SSM_P).reshape(1, SSM_DI)
    dtb = ssm_dt_bias.reshape(1, SSM_H)
    alog = ssm_a_log.reshape(1, SSM_H)
    cbias = ssm_conv_b.reshape(1, SSM_CONV_DIM)
    bst = jnp.transpose(gm_b_s)

    t8 = np.arange(8) % DEC_SEQ
    wj = []
    for j in range(DEC_SEQ):
        src = t8 - j
        vals = gm_w_s[:, t8, np.maximum(src, 0)]
        vals = jnp.where(jnp.asarray(src >= 0)[None, :], vals, 0.0)
        wj.append(jnp.repeat(jnp.transpose(vals), 128, axis=1))
    wj = jnp.stack(wj)
    bt = jnp.repeat(jnp.transpose(gm_b_s[:, t8]), 128, axis=1)

    pool_sel = jnp.asarray(_pool_selectors(), bf16)
    conv_sel = jnp.asarray(_conv_selectors(), bf16)

    xp = x_prompt.reshape(nbp * seq, D)
    xs = x_sample.reshape(nbs * dseq, D)
    pp = p_prompt.reshape(DEPTH, nbp * seq, D_PLE)
    ps = p_sample.reshape(DEPTH, nbs * dseq, D_PLE)
    hist2d = state_pool_l1.reshape(nbs * POOL_HIST, D)
    cs2d = state_conv_l3.reshape(nbs * 3, SSM_CONV_DIM)
    ssm_s0 = state_ssm_l3.reshape(nbs, SSM_DI, SSM_N)

    outs = {}
    for i in range(DEPTH):
        xp, xs = _ffn(xp, xs, i, n1g, w1g, w1u, w1d)
        gmix = norm_mix[i]
        if i == 0:
            xp = _gmlp_prompt(xp, gmix, gm_in, gm_ln, gm_w_s, bst, gm_out)
            xs, chunk_v = _gmlp_sample(xs, gmix, gm_in, gm_ln, wj, bt, gm_out)
            outs["chunk_v"] = chunk_v.reshape(nbs, dseq, D)
        elif i == 1:
            xp, outs["pool_p"] = _pool_prompt(xp, gmix, pw, pool_scale, nbp)
            xs, hs = _pool_sample(xs, hist2d, gmix, pw, pool_scale, pool_sel, PAST_LEN)
            outs["pool_s"] = jnp.concatenate([state_pool_l1[:, dseq:], hs.reshape(nbs, dseq, D)], axis=1)
        elif i == 2:
            xp, outs["gla_p"] = _gla_prompt(xp, gmix, gla_in, gla_a1, gla_a2, gla_b_a, gla_norm, gla_out, nbp)
            qkvr, la = _gla_pre(xs, gmix, gla_in, gla_a1, gla_a2, gla_b_a)
            o, outs["gla_s"] = _gla_sample(qkvr, la, state_gla_l2)
            xs = _gla_post(o, qkvr, xs, gla_norm, gla_out)
        else:
            xp, sn, outs["conv_p"] = _ssd_prompt(xp, gmix, ssm_in, ssm_wdt, ssm_conv_w, cbias, lane_pad(dtb),
                                                 lane_pad(alog), dx, ssm_norm, ssm_out, nbp)
            outs["ssm_p"] = sn.reshape(nbp, SSM_H, SSM_P, SSM_N)
            z, xbc, dtr = _ssd_pre(xs, gmix, ssm_in)
            y, sn = _ssd_sample(xbc, dtr, cs2d, ssm_s0, ssm_conv_w, cbias, dtb, lane_pad(alog), dx, conv_sel)
            outs["ssm_s"] = sn.reshape(nbs, SSM_H, SSM_P, SSM_N)
            outs["conv_s"] = xbc.reshape(nbs, dseq, SSM_CONV_DIM)[:, dseq - 3:]
            xs = _ssd_post(y, z, xs, ssm_norm, ssm_out)
        fin = norm_final if i == DEPTH - 1 else None
        xp, xs = _ffn(xp, xs, i, n2g, w2g, w2u, w2d, ple=(pp, ps, npg, wpg, wpp), final_g=fin)

    return (xp.reshape(nbp, seq, D), xs.reshape(nbs, dseq, D), outs["chunk_v"], outs["pool_p"], outs["pool_s"],
            outs["gla_p"], outs["gla_s"], outs["ssm_p"], outs["ssm_s"], outs["conv_p"], outs["conv_s"])
```

```python
import functools

import jax
import jax.numpy as jnp
import numpy as np
from jax import lax
from jax.experimental import pallas as pl
from jax.experimental.pallas import tpu as pltpu

f32 = jnp.float32
bf16 = jnp.bfloat16

D = 1024
F = 2816
FC = 256
NF = F // FC
D_PLE = 256
EPS = 1e-6
DEPTH = 4

GM_CHUNK = 128
GM_GROUPS = 8
POOL_WINDOWS = (2, 4, 8, 16)
POOL_GD = D // 4
POOL_HIST = 15

GLA_H = 4
GLA_DK = 128
GLA_DV = 256
GLA_QK = GLA_H * GLA_DK
GLA_V = GLA_H * GLA_DV
GLA_NORMALIZER = 16.0
GLA_C = 128

SSM_DI = 2048
SSM_P = 64
SSM_H = 32
SSM_G = 4
SSM_HPG = 8
SSM_N = 128
SSM_CONV_DIM = SSM_DI + 2 * SSM_G * SSM_N
SSM_C = 128
GD = SSM_HPG * SSM_P

DEC_SEQ = 4
PAST_LEN = 16384

VMEM_LIMIT = 60 * 1024 * 1024


def _dot(a, b):
    return jnp.dot(a, b, preferred_element_type=f32)


def _dot_nt(a, b):
    return lax.dot_general(a, b, (((1,), (1,)), ((), ())), preferred_element_type=f32)


def _dot_tn(a, b):
    return lax.dot_general(a, b, (((0,), (0,)), ((), ())), preferred_element_type=f32)


def _dot_hi(a, b):
    return jnp.dot(a, b, precision=lax.Precision.HIGHEST, preferred_element_type=f32)


def _split3(x):
    hi = x.astype(bf16)
    r1 = x - hi.astype(f32)
    mid = r1.astype(bf16)
    lo = (r1 - mid.astype(f32)).astype(bf16)
    return hi, mid, lo


def _sel_dot(sel, parts):
    hi, mid, lo = parts
    return _dot(sel, hi) + _dot(sel, mid) + _dot(sel, lo)


def _rms(x, g):
    return x * lax.rsqrt(jnp.mean(x * x, axis=-1, keepdims=True) + EPS) * g


def _sigmoid(x):
    return 0.5 + 0.5 * jnp.tanh(0.5 * x)


def _silu(x):
    hx = 0.5 * x
    return hx + hx * jnp.tanh(hx)


def _cspec(shape):
    nd = len(shape)
    return pl.BlockSpec(shape, lambda *_: (0,) * nd, pipeline_mode=pl.Buffered(1))


def _params(*sem):
    return pltpu.CompilerParams(dimension_semantics=tuple(sem), vmem_limit_bytes=VMEM_LIMIT)


def _iota2(shape, dim):
    return lax.broadcasted_iota(jnp.int32, shape, dim)


def _two_stage(step, bufs0, bufs1):
    par = lax.rem(pl.program_id(0), 2)

    @pl.when(par == 0)
    def _():
        step(bufs0, bufs1)

    @pl.when(par == 1)
    def _():
        step(bufs1, bufs0)


def _stage_specs(tm, nt):
    a = pl.BlockSpec((tm, D), lambda s: (jnp.minimum(s, nt - 1), 0))
    b = pl.BlockSpec((tm, D), lambda s: (jnp.clip(s - 1, 0, nt - 1), 0))
    return a, b


def _ffn_kernel(*refs, ple, final, n_prompt):
    xp_ref, xs_ref, g_ref, wg_ref, wu_ref, wd_ref = refs[:6]
    k = 6
    if ple:
        pp_ref, ps_ref, gp_ref, wpg_ref, wpp_ref = refs[k:k + 5]
        k += 5
    if final:
        gf_ref = refs[k]
        k += 1
    op_ref, os_ref, h_scr, acc_scr = refs[k:k + 4]

    is_prompt = pl.program_id(0) < n_prompt
    x = jnp.where(is_prompt, xp_ref[...], xs_ref[...])
    h_scr[...] = _rms(x, g_ref[...]).astype(bf16)
    for j in range(NF):
        sl = slice(j * FC, (j + 1) * FC)
        hb = h_scr[...]
        a = _silu(_dot(hb, wg_ref[:, sl].astype(bf16))) * _dot(hb, wu_ref[:, sl].astype(bf16))
        contrib = _dot(a.astype(bf16), wd_ref[sl, :].astype(bf16))
        if j == 0:
            acc_scr[...] = contrib
        else:
            acc_scr[...] += contrib
    y = x + 0.5 * acc_scr[...]
    if ple:
        h2 = _rms(y, gp_ref[...]).astype(bf16)
        gate = _sigmoid(_dot(h2, wpg_ref[...]))
        p = jnp.where(is_prompt, pp_ref[...], ps_ref[...]).astype(bf16)
        y = y + gate * _dot(p, wpp_ref[...])
    if final:
        y = _rms(y, gf_ref[...])

    @pl.when(is_prompt)
    def _():
        op_ref[...] = y

    @pl.when(jnp.logical_not(is_prompt))
    def _():
        os_ref[...] = y


def _lspec(shape, layer):
    nd = len(shape)
    return pl.BlockSpec((None,) + tuple(shape), lambda *_: (layer,) + (0,) * nd,
                        pipeline_mode=pl.Buffered(1))


def _ffn(xp, xs, layer, g, wg, wu, wd, ple=None, final_g=None):
    tm = xs.shape[0]
    n_prompt = xp.shape[0] // tm
    prow = lambda w: pl.BlockSpec((tm, w), lambda i: (jnp.minimum(i, n_prompt - 1), 0))
    srow = lambda w: pl.BlockSpec((tm, w), lambda i: (0, 0), pipeline_mode=pl.Buffered(1))
    in_specs = [prow(D), srow(D), _lspec((1, D), layer), _lspec((D, F), layer), _lspec((D, F), layer),
                _lspec((F, D), layer)]
    args = [xp, xs, g, wg, wu, wd]
    if ple is not None:
        pp, ps, gp, wpg, wpp = ple
        in_specs += [pl.BlockSpec((None, tm, D_PLE), lambda i: (layer, jnp.minimum(i, n_prompt - 1), 0)),
                     pl.BlockSpec((None, tm, D_PLE), lambda i: (layer, 0, 0), pipeline_mode=pl.Buffered(1)),
                     _lspec((1, D), layer), _lspec((D, D), layer), _lspec((D_PLE, D), layer)]
        args += [pp, ps, gp, wpg, wpp]
    if final_g is not None:
        in_specs.append(_cspec((1, D)))
        args.append(final_g.reshape(1, D))
    return pl.pallas_call(
        functools.partial(_ffn_kernel, ple=ple is not None, final=final_g is not None, n_prompt=n_prompt),
        out_shape=(jax.ShapeDtypeStruct(xp.shape, f32), jax.ShapeDtypeStruct(xs.shape, f32)),
        grid=(n_prompt + 1,),
        in_specs=in_specs,
        out_specs=(prow(D), pl.BlockSpec((tm, D), lambda i: (0, 0))),
        scratch_shapes=[pltpu.VMEM((tm, D), bf16), pltpu.VMEM((tm, D), f32)],
        compiler_params=_params("arbitrary"),
        name="ffn",
    )(*args)


def _gelu(x):
    c = 0.7978845608028654
    hx = 0.5 * x
    return hx + hx * jnp.tanh(x * (c + (c * 0.044715) * (x * x)))


def _gmlp_act(y, ln_ref):
    y = _gelu(y)
    u = y[:, :D]
    v = y[:, D:]
    vc = v - jnp.mean(v, axis=-1, keepdims=True)
    v = vc * lax.rsqrt(jnp.mean(vc * vc, axis=-1, keepdims=True) + EPS) * ln_ref[...]
    return u, v


def _gmlp_pre(x, g, win_ref, ln_ref):
    h = _rms(x, g).astype(bf16)
    return _gmlp_act(_dot(h, win_ref[...]), ln_ref)


def _gmlp_prompt_kernel(x_ref, g_ref, win_ref, ln_ref, ws_ref, bst_ref, wout_ref, o_ref,
                        u_scr, vb_scr, t_scr):
    tm = x_ref.shape[0]
    x = x_ref[...]
    t_scr[...] = _rms(x, g_ref[...]).astype(bf16)
    cbw = D // 2
    gelu_block = lambda c0: _gelu(_dot(t_scr[...], win_ref[:, c0:c0 + cbw]))
    v = jnp.concatenate([gelu_block(D + i * cbw) for i in range(2)], axis=1)
    vc = v - jnp.mean(v, axis=-1, keepdims=True)
    v = vc * lax.rsqrt(jnp.mean(vc * vc, axis=-1, keepdims=True) + EPS) * ln_ref[...]
    vb_scr[...] = v.astype(bf16)
    for i in range(2):
        u_scr[:, i * cbw:(i + 1) * cbw] = gelu_block(i * cbw)
    causal = _iota2((GM_CHUNK, GM_CHUNK), 0) >= _iota2((GM_CHUNK, GM_CHUNK), 1)
    for g in range(GM_GROUPS):
        wm = jnp.where(causal, ws_ref[g], 0.0).astype(bf16)
        bias = bst_ref[:, g:g + 1]
        cs = slice(g * 128, (g + 1) * 128)
        for ch in range(tm // GM_CHUNK):
            rs = slice(ch * GM_CHUNK, (ch + 1) * GM_CHUNK)
            sv = _dot(wm, vb_scr[rs, cs]) + bias
            t_scr[rs, cs] = (u_scr[rs, cs] * sv).astype(bf16)
    o_ref[...] = x + _dot(t_scr[...], wout_ref[...])


def _gmlp_prompt(x, g, win, ln, ws, bst, wout, tm=512):
    T = x.shape[0]
    row = pl.BlockSpec((tm, D), lambda i: (i, 0))
    return pl.pallas_call(
        _gmlp_prompt_kernel,
        out_shape=jax.ShapeDtypeStruct((T, D), f32),
        grid=(T // tm,),
        in_specs=[row, _cspec((1, D)), _cspec((D, 2 * D)), _cspec((1, D)),
                  _cspec((GM_GROUPS, GM_CHUNK, GM_CHUNK)), _cspec((GM_CHUNK, GM_GROUPS)),
                  _cspec((D, D))],
        out_specs=row,
        scratch_shapes=[pltpu.VMEM((tm, D), f32), pltpu.VMEM((tm, D), bf16), pltpu.VMEM((tm, D), bf16)],
        compiler_params=_params("arbitrary"),
        name="gmlp_prompt",
    )(x, g.reshape(1, D), win, ln.reshape(1, D), ws, bst, wout)


def _gmlp_sample_kernel(x_ref, g_ref, win_ref, ln_ref, wj_ref, bt_ref, wout_ref, o_ref, v_ref):
    rows = x_ref.shape[0]
    n = rows // 8
    x = x_ref[...]
    u, v = _gmlp_pre(x, g_ref[...], win_ref, ln_ref)
    v_ref[...] = v
    acc = v.reshape(n, 8, D) * wj_ref[0][None]
    for j in range(1, DEC_SEQ):
        acc = acc + pltpu.roll(v, j, axis=0).reshape(n, 8, D) * wj_ref[j][None]
    sv = (acc + bt_ref[...][None]).reshape(rows, D)
    o_ref[...] = x + _dot((u * sv).astype(bf16), wout_ref[...])


def _gmlp_sample(x, g, win, ln, wj, bt, wout):
    T = x.shape[0]
    full = pl.BlockSpec((T, D), lambda i: (0, 0))
    return pl.pallas_call(
        _gmlp_sample_kernel,
        out_shape=(jax.ShapeDtypeStruct((T, D), f32), jax.ShapeDtypeStruct((T, D), f32)),
        grid=(1,),
        in_specs=[full, _cspec((1, D)), _cspec((D, 2 * D)), _cspec((1, D)),
                  _cspec((DEC_SEQ, 8, D)), _cspec((8, D)), _cspec((D, D))],
        out_specs=(full, full),
        compiler_params=_params("arbitrary"),
        name="gmlp_sample",
    )(x, g.reshape(1, D), win, ln.reshape(1, D), wj, bt, wout)


def _pool_prompt_kernel(x_ref, g_ref, pw_ref, sc_ref, o_ref, pn_ref, ext_scr, *, nl):
    tm = x_ref.shape[0]
    l = pl.program_id(1)

    @pl.when(l == 0)
    def _():
        ext_scr[0:16, :] = jnp.zeros((16, D), f32)

    x = x_ref[...]
    ext_scr[16:16 + tm, :] = _rms(x, g_ref[...])
    pos = l * tm + _iota2((tm, 1), 0)
    for gi, w in enumerate(POOL_WINDOWS):
        cs = slice(gi * POOL_GD, (gi + 1) * POOL_GD)
        s = ext_scr[:, cs]
        k = 1
        while k < w:
            s = s + pltpu.roll(s, k, axis=0)
            k *= 2
        s = s[16:16 + tm, :]
        h = ext_scr[16:16 + tm, cs]
        cnt = jnp.minimum(pos + 1, w).astype(f32)
        diff = (s / cnt - h).astype(bf16)
        o_ref[:, cs] = x[:, cs] + _dot(diff, pw_ref[gi]) * sc_ref[:, cs]

    @pl.when(l == nl - 1)
    def _():
        pn_ref[0] = ext_scr[16 + tm - POOL_HIST:16 + tm, :]

    ext_scr[0:16, :] = ext_scr[tm:tm + 16, :]


def _pool_prompt(x, g, pw, sc, nb, tm=512):
    T = x.shape[0]
    nl = T // nb // tm
    row = pl.BlockSpec((tm, D), lambda b, l: (b * nl + l, 0))
    return pl.pallas_call(
        functools.partial(_pool_prompt_kernel, nl=nl),
        out_shape=(jax.ShapeDtypeStruct((T, D), f32), jax.ShapeDtypeStruct((nb, POOL_HIST, D), f32)),
        grid=(nb, nl),
        in_specs=[row, _cspec((1, D)), _cspec((4, POOL_GD, POOL_GD)), _cspec((1, D))],
        out_specs=(row, pl.BlockSpec((1, POOL_HIST, D), lambda b, l: (b, 0, 0))),
        scratch_shapes=[pltpu.VMEM((tm + 16, D), f32)],
        compiler_params=_params("arbitrary", "arbitrary"),
        name="pool_prompt",
    )(x, g.reshape(1, D), pw, sc.reshape(1, D))


_SB = 8


def _pool_sample_kernel(x_ref, hist_ref, g_ref, pw_ref, sc_ref, sel_ref, o_ref, pn_ref, hp_scr, *, pos0):
    rows = x_ref.shape[0]
    nseq = hist_ref.shape[0]
    x = x_ref[...]
    h = _rms(x, g_ref[...])
    hp_scr[...] = jnp.zeros_like(hp_scr)
    for s in range(nseq):
        hp_scr[16 * s:16 * s + POOL_HIST, :] = hist_ref[s]
        pn_ref[s, 0:POOL_HIST - DEC_SEQ, :] = hist_ref[s, DEC_SEQ:POOL_HIST, :]
        pn_ref[s, POOL_HIST - DEC_SEQ:POOL_HIST, :] = h[DEC_SEQ * s:DEC_SEQ * (s + 1), :]
    t = _iota2((rows, 1), 0) & (DEC_SEQ - 1)
    parts = _split3(hp_scr[...])
    for gi, w in enumerate(POOL_WINDOWS):
        cs = slice(gi * POOL_GD, (gi + 1) * POOL_GD)
        hc = h[:, cs]
        s = hc
        for j in range(1, min(w, DEC_SEQ)):
            s = s + jnp.where(t >= j, pltpu.roll(hc, j, axis=0), 0.0)
        s = s + _sel_dot(sel_ref[gi], tuple(p[:, cs] for p in parts))
        cnt = jnp.minimum(pos0 + t + 1, w).astype(f32)
        diff = (s / cnt - hc).astype(bf16)
        o_ref[:, cs] = x[:, cs] + _dot(diff, pw_ref[gi]) * sc_ref[:, cs]


def _pool_sample(x, hist, g, pw, sc, sel, pos0):
    T = x.shape[0]
    rb = _SB * DEC_SEQ
    row = pl.BlockSpec((rb, D), lambda i: (i, 0))
    seq = pl.BlockSpec((_SB, POOL_HIST, D), lambda i: (i, 0, 0))
    return pl.pallas_call(
        functools.partial(_pool_sample_kernel, pos0=pos0),
        out_shape=(jax.ShapeDtypeStruct((T, D), f32), jax.ShapeDtypeStruct(hist.shape, f32)),
        grid=(T // rb,),
        in_specs=[row, seq, _cspec((1, D)), _cspec((4, POOL_GD, POOL_GD)), _cspec((1, D)),
                  _cspec((4, rb, _SB * 16))],
        out_specs=(row, seq),
        scratch_shapes=[pltpu.VMEM((_SB * 16, D), f32)],
        compiler_params=_params("arbitrary"),
        name="pool_sample",
    )(x, hist, g.reshape(1, D), pw, sc.reshape(1, D), sel)


def _cumsum_rows(x, seg=None):
    n = x.shape[0]
    seg = n if seg is None else seg
    pos = _iota2((n, 1), 0) & (seg - 1)
    s = 1
    while s < seg:
        x = x + jnp.where(pos >= s, pltpu.roll(x, s, axis=0), 0.0)
        s *= 2
    return x


def _gla_pre_body(x, g_ref, win_ref, wa1_ref, wa2_ref, ba_ref, qkvr_ref, la_ref):
    h = _rms(x, g_ref[...]).astype(bf16)
    qkvr_ref[...] = _dot(h, win_ref[...])
    a1 = _dot(h, wa1_ref[...]).astype(bf16)
    z = _dot(a1, wa2_ref[...]) + ba_ref[...]
    la_ref[...] = jax.nn.log_sigmoid(z) / GLA_NORMALIZER


def _gla_post_body(o, r, x, gn_ref, wout_ref):
    parts = []
    for hd in range(GLA_H):
        vs = slice(hd * GLA_DV, (hd + 1) * GLA_DV)
        parts.append(_rms(o[:, vs], gn_ref[:, vs]))
    on = jnp.concatenate(parts, axis=1)
    gated = (on * _silu(r)).astype(bf16)
    return x + _dot(gated, wout_ref[...])


def _gla_pre_kernel(x_ref, g_ref, win_ref, wa1_ref, wa2_ref, ba_ref, qkvr_ref, la_ref):
    _gla_pre_body(x_ref[...], g_ref, win_ref, wa1_ref, wa2_ref, ba_ref, qkvr_ref, la_ref)


def _gla_pre(x, g, win, wa1, wa2, ba, tm=512):
    T = x.shape[0]
    W = 2 * GLA_QK + 2 * GLA_V
    return pl.pallas_call(
        _gla_pre_kernel,
        out_shape=(jax.ShapeDtypeStruct((T, W), f32), jax.ShapeDtypeStruct((T, GLA_QK), f32)),
        grid=(T // tm,),
        in_specs=[pl.BlockSpec((tm, D), lambda i: (i, 0)), _cspec((1, D)), _cspec((D, W)),
                  _cspec((D, 16)), _cspec((16, GLA_QK)), _cspec((1, GLA_QK))],
        out_specs=(pl.BlockSpec((tm, W), lambda i: (i, 0)), pl.BlockSpec((tm, GLA_QK), lambda i: (i, 0))),
        compiler_params=_params("arbitrary"),
        name="gla_pre",
    )(x, g.reshape(1, D), win, wa1, wa2, ba.reshape(1, GLA_QK))


def _gla_chunk(q, k, v, b, bl, anc, mask, segs, C, halves=False):
    qs = q * (GLA_DK ** -0.5)
    q_dec = (qs * jnp.exp(b)).astype(bf16)
    k_end = k * jnp.exp(bl - b)
    q_mid = (qs * jnp.exp(b - anc)).astype(bf16)
    k_mid = (k * jnp.exp(anc - b)).astype(bf16)
    ebl = jnp.exp(bl)
    vb = v.astype(bf16)
    rid = _iota2((C, 1), 0)
    if halves:
        hc = C // 2
        edge = b[hc - 1:hc, :]
        q_x = jnp.where(rid >= hc, qs * jnp.exp(b - edge), 0.0).astype(bf16)
        k_x = jnp.where(rid < hc, k * jnp.exp(edge - b), 0.0).astype(bf16)
        mask = mask & ((_iota2((C, C), 0) >= hc) == (_iota2((C, C), 1) >= hc))
    outs = []
    for hd in range(GLA_H):
        ds_ = slice(hd * GLA_DK, (hd + 1) * GLA_DK)
        vs = slice(hd * GLA_DV, (hd + 1) * GLA_DV)
        sc = jnp.where(mask, _dot_nt(q_mid[:, ds_], k_mid[:, ds_]), 0.0)
        if halves:
            sc = sc + _dot_nt(q_x[:, ds_], k_x[:, ds_])
        o = _dot(sc.astype(bf16), vb[:, vs])
        o_int = None
        for (lo, hi, s_in, s_out) in segs:
            oi = _dot(q_dec[:, ds_], s_in[hd].astype(bf16))
            o_int = oi if o_int is None else jnp.where(rid >= lo, oi, o_int)
        outs.append(o + o_int)
        for (lo, hi, s_in, s_out) in segs:
            if ebl.shape[0] == 1:
                dec = jnp.transpose(jnp.broadcast_to(ebl[:, ds_], (8, GLA_DK)))[:, 0:1]
            else:
                r0 = max(hi - 8, 0)
                dec = jnp.transpose(ebl[r0:r0 + 8, ds_])[:, hi - 1 - r0:hi - r0]
            ke = k_end[:, ds_]
            if len(segs) > 1:
                ke = jnp.where((rid >= lo) & (rid < hi), ke, 0.0)
            s_out[hd] = s_in[hd] * dec + _dot_tn(ke.astype(bf16), vb[:, vs])
    return jnp.concatenate(outs, axis=1)


def _gla_prompt_kernel(xa_ref, xb_ref, g_ref, win_ref, wa1_ref, wa2_ref, ba_ref, gn_ref, wout_ref,
                       out_ref, sn_ref, q0, l0, q1, l1, o_scr, s_scr, *, nl):
    tm = xa_ref.shape[0]
    C = GLA_C
    s = pl.program_id(0)
    tpos = lax.rem(s - 1, nl)

    @pl.when(s == 0)
    def _():
        q1[...] = jnp.zeros_like(q1)
        l1[...] = jnp.zeros_like(l1)

    @pl.when((s == 0) | (tpos == 0))
    def _():
        s_scr[...] = jnp.zeros_like(s_scr)

    def step(bw, br):
        qw, lw = bw
        qr, lr = br
        _gla_pre_body(xa_ref[...], g_ref, win_ref, wa1_ref, wa2_ref, ba_ref, qw, lw)
        tri = _iota2((C, C), 0) >= _iota2((C, C), 1)
        for c in range(tm // C):
            rs = slice(c * C, (c + 1) * C)
            b = _cumsum_rows(lr[rs, :])
            bl = b[C - 1:C, :]
            anc = jnp.where(_iota2((C, 1), 0) < C // 2, b[C // 4 - 1:C // 4, :], b[3 * C // 4 - 1:3 * C // 4, :])
            q = qr[rs, 0:GLA_QK]
            k = qr[rs, GLA_QK:2 * GLA_QK]
            v = qr[rs, 2 * GLA_QK:2 * GLA_QK + GLA_V]
            o_scr[rs, :] = _gla_chunk(q, k, v, b, bl, anc, tri, [(0, C, s_scr, s_scr)], C, halves=True)
        out_ref[...] = _gla_post_body(o_scr[...], qr[:, 2 * GLA_QK + GLA_V:], xb_ref[...], gn_ref, wout_ref)

    _two_stage(step, (q0, l0), (q1, l1))

    @pl.when(tpos == nl - 1)
    def _():
        sn_ref[0] = s_scr[...]


def _gla_prompt(x, g, win, wa1, wa2, ba, gn, wout, nb, tm=256):
    T = x.shape[0]
    nt = T // tm
    nl = nt // nb
    W = 2 * GLA_QK + 2 * GLA_V
    a, b = _stage_specs(tm, nt)
    return pl.pallas_call(
        functools.partial(_gla_prompt_kernel, nl=nl),
        out_shape=(jax.ShapeDtypeStruct((T, D), f32),
                   jax.ShapeDtypeStruct((nb, GLA_H, GLA_DK, GLA_DV), f32)),
        grid=(nt + 1,),
        in_specs=[a, b, _cspec((1, D)), _cspec((D, W)), _cspec((D, 16)), _cspec((16, GLA_QK)),
                  _cspec((1, GLA_QK)), _cspec((1, GLA_V)), _cspec((GLA_V, D))],
        out_specs=(b, pl.BlockSpec((1, GLA_H, GLA_DK, GLA_DV), lambda s: (jnp.maximum(s - 1, 0) // nl, 0, 0, 0))),
        scratch_shapes=[pltpu.VMEM((tm, W), f32), pltpu.VMEM((tm, GLA_QK), f32),
                        pltpu.VMEM((tm, W), f32), pltpu.VMEM((tm, GLA_QK), f32),
                        pltpu.VMEM((tm, GLA_V), f32), pltpu.VMEM((GLA_H, GLA_DK, GLA_DV), f32)],
        compiler_params=_params("arbitrary"),
        name="gla_prompt",
    )(x, x, g.reshape(1, D), win, wa1, wa2, ba.reshape(1, GLA_QK), gn.reshape(1, GLA_V), wout)


_GLA_SB = 16


def _gla_sample_kernel(qkv_ref, la_ref, s0_ref, o_ref, sn_ref, b_scr, bl_scr):
    rows = qkv_ref.shape[0]
    r = _iota2((rows, rows), 0)
    c = _iota2((rows, rows), 1)
    same = (r >> 2) == (c >> 2)
    la = la_ref[...]
    b_scr[...] = _dot_hi((same & (r >= c)).astype(f32), la)
    bl_scr[...] = _dot_hi(same.astype(f32), la)
    r8 = _iota2((8, 8), 0)
    c8 = _iota2((8, 8), 1)
    mask8 = ((r8 >> 2) == (c8 >> 2)) & (r8 >= c8)

    def body(p, carry):
        rs = pl.ds(pl.multiple_of(p * 8, 8), 8)
        q = qkv_ref[rs, 0:GLA_QK]
        k = qkv_ref[rs, GLA_QK:2 * GLA_QK]
        v = qkv_ref[rs, 2 * GLA_QK:2 * GLA_QK + GLA_V]
        segs = [(0, 4, s0_ref.at[2 * p], sn_ref.at[2 * p]),
                (4, 8, s0_ref.at[2 * p + 1], sn_ref.at[2 * p + 1])]
        o_ref[rs, :] = _gla_chunk(q, k, v, b_scr[rs, :], bl_scr[rs, :], 0.0, mask8, segs, 8)
        return carry

    lax.fori_loop(0, rows // 8, body, 0)


def _gla_sample(qkvr, la, s0):
    T = qkvr.shape[0]
    nb = s0.shape[0]
    rb = _GLA_SB * DEC_SEQ
    W = 2 * GLA_QK + GLA_V
    st = pl.BlockSpec((_GLA_SB, GLA_H, GLA_DK, GLA_DV), lambda i: (i, 0, 0, 0))
    return pl.pallas_call(
        _gla_sample_kernel,
        out_shape=(jax.ShapeDtypeStruct((T, GLA_V), f32),
                   jax.ShapeDtypeStruct((nb, GLA_H, GLA_DK, GLA_DV), f32)),
        grid=(T // rb,),
        in_specs=[pl.BlockSpec((rb, W), lambda i: (i, 0)), pl.BlockSpec((rb, GLA_QK), lambda i: (i, 0)), st],
        out_specs=(pl.BlockSpec((rb, GLA_V), lambda i: (i, 0)), st),
        scratch_shapes=[pltpu.VMEM((rb, GLA_QK), f32), pltpu.VMEM((rb, GLA_QK), f32)],
        compiler_params=_params("arbitrary"),
        name="gla_sample",
    )(qkvr, la, s0)


def _gla_post_kernel(o_ref, r_ref, x_ref, gn_ref, wout_ref, out_ref):
    out_ref[...] = _gla_post_body(o_ref[...], r_ref[...], x_ref[...], gn_ref, wout_ref)


def _gla_post(o, qkvr, x, gn, wout, tm=512):
    T = x.shape[0]
    row = pl.BlockSpec((tm, D), lambda i: (i, 0))
    return pl.pallas_call(
        _gla_post_kernel,
        out_shape=jax.ShapeDtypeStruct((T, D), f32),
        grid=(T // tm,),
        in_specs=[row, pl.BlockSpec((tm, GLA_V), lambda i: (i, 2)), row, _cspec((1, GLA_V)),
                  _cspec((GLA_V, D))],
        out_specs=row,
        compiler_params=_params("arbitrary"),
        name="gla_post",
    )(o, qkvr, x, gn.reshape(1, GLA_V), wout)


def _ssd_pre_kernel(x_ref, g_ref, win_ref, z_ref, xbc_ref, dt_ref):
    h = _rms(x_ref[...], g_ref[...]).astype(bf16)
    z_ref[...] = _dot(h, win_ref[:, 0:SSM_DI])
    xbc_ref[...] = _dot(h, win_ref[:, SSM_DI:SSM_DI + SSM_CONV_DIM])
    dt_ref[...] = _dot(h, win_ref[:, SSM_DI + SSM_CONV_DIM:])


def _ssd_pre(x, g, win, tm=512):
    T = x.shape[0]
    return pl.pallas_call(
        _ssd_pre_kernel,
        out_shape=(jax.ShapeDtypeStruct((T, SSM_DI), f32), jax.ShapeDtypeStruct((T, SSM_CONV_DIM), f32),
                   jax.ShapeDtypeStruct((T, SSM_H), f32)),
        grid=(T // tm,),
        in_specs=[pl.BlockSpec((tm, D), lambda i: (i, 0)), _cspec((1, D)), _cspec((D, win.shape[1]))],
        out_specs=(pl.BlockSpec((tm, SSM_DI), lambda i: (i, 0)),
                   pl.BlockSpec((tm, SSM_CONV_DIM), lambda i: (i, 0)),
                   pl.BlockSpec((tm, SSM_H), lambda i: (i, 0))),
        compiler_params=_params("arbitrary"),
        name="ssd_pre",
    )(x, g.reshape(1, D), win)


def _conv_silu(terms, cw_ref, cb_ref):
    acc = terms[0] * cw_ref[0:1, :]
    for j in range(1, 4):
        acc = acc + terms[j] * cw_ref[j:j + 1, :]
    return _silu(cb_ref[...] + acc)


def _ssd_chunk(act_ref, rs, dtc, a_row, dx_ref, y_ref, mask, segs, C):
    rid = _iota2((C, 1), 0)
    cum = _cumsum_rows(dtc * a_row, segs[0][1] - segs[0][0])
    cum_last = cum[segs[-1][1] - 1:segs[-1][1], :]
    for (lo, hi, _, _) in segs[-2::-1]:
        cum_last = jnp.where(rid < hi, cum[hi - 1:hi, :], cum_last)
    e_cum = jnp.exp(cum)
    wend = jnp.exp(cum_last - cum) * dtc
    cum_t = jnp.transpose(cum)
    dt_t = jnp.transpose(dtc)
    dec_t = jnp.exp(cum_t)
    lane_lo = _iota2((C, 2 * SSM_P), 1) < SSM_P
    for g in range(SSM_G):
        gs = slice(g * GD, (g + 1) * GD)
        bm = act_ref[rs, SSM_DI + g * SSM_N:SSM_DI + (g + 1) * SSM_N].astype(bf16)
        cm = act_ref[rs, SSM_DI + SSM_G * SSM_N + g * SSM_N:SSM_DI + SSM_G * SSM_N + (g + 1) * SSM_N].astype(bf16)
        cb = _dot_nt(cm, bm)
        y_int = None
        for (lo, hi, s_in, s_out) in segs:
            yi = _dot_nt(cm, s_in[gs, :].astype(bf16))
            y_int = yi if y_int is None else jnp.where(rid >= lo, yi, y_int)
        xw = []
        for pr in range(SSM_HPG // 2):
            h0 = g * SSM_HPG + 2 * pr
            cols = slice(g * GD + 2 * pr * SSM_P, g * GD + (2 * pr + 2) * SSM_P)
            xs = act_ref[rs, cols]
            e_pair = jnp.where(lane_lo, e_cum[:, h0:h0 + 1], e_cum[:, h0 + 1:h0 + 2])
            w_pair = jnp.where(lane_lo, wend[:, h0:h0 + 1], wend[:, h0 + 1:h0 + 2])
            acc = y_int[:, 2 * pr * SSM_P:(2 * pr + 2) * SSM_P] * e_pair + xs * dx_ref[:, cols]
            xp = xs.astype(bf16)
            for hh, keep in ((h0, lane_lo), (h0 + 1, ~lane_lo)):
                seg = cum[:, hh:hh + 1] - cum_t[hh:hh + 1, :]
                mix = cb * jnp.exp(jnp.where(mask, seg, -jnp.inf)) * dt_t[hh:hh + 1, :]
                acc = acc + _dot(mix.astype(bf16), jnp.where(keep, xp, jnp.zeros_like(xp)))
            y_ref[rs, cols] = acc
            xw.append(xs * w_pair)
        xw = jnp.concatenate(xw, axis=1)
        for (lo, hi, s_in, s_out) in segs:
            xws = xw if len(segs) == 1 else jnp.where((rid >= lo) & (rid < hi), xw, 0.0)
            upd = _dot_tn(xws.astype(bf16), bm)
            for r in range(SSM_HPG):
                hh = g * SSM_HPG + r
                hs = slice(hh * SSM_P, (hh + 1) * SSM_P)
                s_out[hs, :] = s_in[hs, :] * dec_t[hh:hh + 1, hi - 1:hi] + upd[r * SSM_P:(r + 1) * SSM_P, :]


def _ssd_post_body(y, z, x, gn_ref, wout_ref):
    y = y * _silu(z)
    parts = []
    for g in range(SSM_G):
        gs = slice(g * GD, (g + 1) * GD)
        parts.append(_rms(y[:, gs], gn_ref[:, gs]))
    yn = jnp.concatenate(parts, axis=1).astype(bf16)
    return x + _dot(yn, wout_ref[...])


def _ssd_prompt_kernel(xa_ref, xb_ref, g_ref, win_ref, wdt_ref, cw_ref, cb_ref, dtb_ref, alog_ref, dx_ref, gn_ref,
                       wout_ref, out_ref, sn_ref, cn_ref, z0, e0, d0, z1, e1, d1, carry_scr, act_scr, y_scr, s_scr,
                       *, nl):
    tm = xa_ref.shape[0]
    C = SSM_C
    s = pl.program_id(0)
    tpos = lax.rem(s - 1, nl)

    @pl.when(s == 0)
    def _():
        z1[...] = jnp.zeros_like(z1)
        e1[...] = jnp.zeros_like(e1)
        d1[...] = jnp.zeros_like(d1)

    @pl.when((s == 0) | (tpos == 0))
    def _():
        carry_scr[...] = jnp.zeros_like(carry_scr)
        s_scr[...] = jnp.zeros_like(s_scr)

    def step(bw, br):
        zw, ew, dw = bw
        zr, er, dr = br
        h = _rms(xa_ref[...], g_ref[...]).astype(bf16)
        zw[...] = _dot(h, win_ref[:, 0:SSM_DI])
        ew[8:8 + tm, :] = _dot(h, win_ref[:, SSM_DI:SSM_DI + SSM_CONV_DIM])
        dw[...] = _dot(h, wdt_ref[...])
        er[0:8, :] = carry_scr[...]
        e = er[...]
        acc = e * cw_ref[0:1, :]
        for j in range(1, 4):
            acc = pltpu.roll(acc, 1, axis=0) + e * cw_ref[j:j + 1, :]
        act_scr[...] = _silu(cb_ref[...] + acc[8:8 + tm, :])
        carry_scr[...] = er[tm:tm + 8, :]
        a_row = -jnp.exp(alog_ref[...])
        tri = _iota2((C, C), 0) >= _iota2((C, C), 1)
        for c in range(tm // C):
            rs = slice(c * C, (c + 1) * C)
            dtc = jax.nn.softplus(dr[rs, :] + dtb_ref[...])
            _ssd_chunk(act_scr, rs, dtc, a_row, dx_ref, y_scr, tri, [(0, C, s_scr, s_scr)], C)
        out_ref[...] = _ssd_post_body(y_scr[...], zr[...], xb_ref[...], gn_ref, wout_ref)

        @pl.when(tpos == nl - 1)
        def _():
            cn_ref[0] = er[8 + tm - 3:8 + tm, :]

    _two_stage(step, (z0, e0, d0), (z1, e1, d1))

    @pl.when(tpos == nl - 1)
    def _():
        sn_ref[0] = s_scr[...]


def _ssd_prompt(x, g, win, wdt, cw, cb, dtb, alog, dx, gn, wout, nb, tm=256):
    T = x.shape[0]
    nt = T // tm
    nl = nt // nb
    WIN = win.shape[1]
    a, b = _stage_specs(tm, nt)
    seq = lambda s: (jnp.maximum(s - 1, 0) // nl, 0, 0)
    bufs = [pltpu.VMEM((tm, SSM_DI), f32), pltpu.VMEM((tm + 8, SSM_CONV_DIM), f32), pltpu.VMEM((tm, 128), f32)]
    return pl.pallas_call(
        functools.partial(_ssd_prompt_kernel, nl=nl),
        out_shape=(jax.ShapeDtypeStruct((T, D), f32),
                   jax.ShapeDtypeStruct((nb, SSM_DI, SSM_N), f32),
                   jax.ShapeDtypeStruct((nb, 3, SSM_CONV_DIM), f32)),
        grid=(nt + 1,),
        in_specs=[a, b, _cspec((1, D)), _cspec((D, WIN)), _cspec((D, 128)),
                  _cspec((4, SSM_CONV_DIM)), _cspec((1, SSM_CONV_DIM)), _cspec((1, 128)),
                  _cspec((1, 128)), _cspec((1, SSM_DI)), _cspec((1, SSM_DI)), _cspec((SSM_DI, D))],
        out_specs=(b, pl.BlockSpec((1, SSM_DI, SSM_N), seq), pl.BlockSpec((1, 3, SSM_CONV_DIM), seq)),
        scratch_shapes=bufs + bufs + [pltpu.VMEM((8, SSM_CONV_DIM), f32), pltpu.VMEM((tm, SSM_CONV_DIM), f32),
                                      pltpu.VMEM((tm, SSM_DI), f32), pltpu.VMEM((SSM_DI, SSM_N), f32)],
        compiler_params=_params("arbitrary"),
        name="ssd_prompt",
    )(x, x, g.reshape(1, D), win, wdt, cw, cb, dtb, alog, dx, gn.reshape(1, SSM_DI), wout)


def _ssd_sample_kernel(xbc_ref, dtr_ref, cs_ref, s0_ref, cw_ref, cb_ref, dtb_ref, alog_ref, dx_ref,
                       sel_ref, y_ref, sn_ref, cn_ref, act_scr, dt_scr, cp_scr):
    rows = xbc_ref.shape[0]
    xb = xbc_ref[...]
    cp_scr[...] = jnp.zeros_like(cp_scr)
    for s in range(cs_ref.shape[0]):
        cp_scr[8 * s:8 * s + 3, :] = cs_ref[s]
        cn_ref[s] = xb[DEC_SEQ * s + 1:DEC_SEQ * (s + 1), :]
    t = _iota2((rows, 1), 0) & (DEC_SEQ - 1)
    parts = _split3(cp_scr[...])
    terms = []
    for j in range(4):
        sh = 3 - j
        term = xb if sh == 0 else jnp.where(t >= sh, pltpu.roll(xb, sh, axis=0), 0.0)
        if j < 3:
            term = term + _sel_dot(sel_ref[j], parts)
        terms.append(term)
    act_scr[...] = _conv_silu(terms, cw_ref, cb_ref)
    dt_scr[...] = jnp.zeros_like(dt_scr)
    dt_scr[:, 0:SSM_H] = jax.nn.softplus(dtr_ref[...] + dtb_ref[...])
    a_row = -jnp.exp(alog_ref[...])
    r8 = _iota2((8, 8), 0)
    c8 = _iota2((8, 8), 1)
    mask8 = ((r8 >> 2) == (c8 >> 2)) & (r8 >= c8)

    def body(p, carry):
        rs = pl.ds(pl.multiple_of(p * 8, 8), 8)
        segs = [(0, 4, s0_ref.at[2 * p], sn_ref.at[2 * p]),
                (4, 8, s0_ref.at[2 * p + 1], sn_ref.at[2 * p + 1])]
        _ssd_chunk(act_scr, rs, dt_scr[rs, :], a_row, dx_ref, y_ref, mask8, segs, 8)
        return carry

    lax.fori_loop(0, rows // 8, body, 0)


def _ssd_sample(xbc, dtr, cs, s0, cw, cb, dtb, alog, dx, sel):
    T = xbc.shape[0]
    nb = s0.shape[0]
    rb = _SB * DEC_SEQ
    st = pl.BlockSpec((_SB, SSM_DI, SSM_N), lambda i: (i, 0, 0))
    cst = pl.BlockSpec((_SB, 3, SSM_CONV_DIM), lambda i: (i, 0, 0))
    return pl.pallas_call(
        _ssd_sample_kernel,
        out_shape=(jax.ShapeDtypeStruct((T, SSM_DI), f32), jax.ShapeDtypeStruct((nb, SSM_DI, SSM_N), f32),
                   jax.ShapeDtypeStruct(cs.shape, f32)),
        grid=(T // rb,),
        in_specs=[pl.BlockSpec((rb, SSM_CONV_DIM), lambda i: (i, 0)), pl.BlockSpec((rb, SSM_H), lambda i: (i, 0)),
                  cst, st,
                  _cspec((4, SSM_CONV_DIM)), _cspec((1, SSM_CONV_DIM)), _cspec((1, SSM_H)),
                  _cspec((1, 128)), _cspec((1, SSM_DI)), _cspec((3, rb, _SB * 8))],
        out_specs=(pl.BlockSpec((rb, SSM_DI), lambda i: (i, 0)), st, cst),
        scratch_shapes=[pltpu.VMEM((rb, SSM_CONV_DIM), f32), pltpu.VMEM((rb, 128), f32),
                        pltpu.VMEM((_SB * 8, SSM_CONV_DIM), f32)],
        compiler_params=_params("arbitrary"),
        name="ssd_sample",
    )(xbc, dtr, cs, s0, cw, cb, dtb, alog, dx, sel)


def _ssd_post_kernel(y_ref, z_ref, x_ref, gn_ref, wout_ref, out_ref):
    out_ref[...] = _ssd_post_body(y_ref[...], z_ref[...], x_ref[...], gn_ref, wout_ref)


def _ssd_post(y, z, x, gn, wout, tm=512):
    T = x.shape[0]
    row = pl.BlockSpec((tm, D), lambda i: (i, 0))
    wide = pl.BlockSpec((tm, SSM_DI), lambda i: (i, 0))
    return pl.pallas_call(
        _ssd_post_kernel,
        out_shape=jax.ShapeDtypeStruct((T, D), f32),
        grid=(T // tm,),
        in_specs=[wide, wide, row, _cspec((1, SSM_DI)), _cspec((SSM_DI, D))],
        out_specs=row,
        compiler_params=_params("arbitrary"),
        name="ssd_post",
    )(y, z, x, gn.reshape(1, SSM_DI), wout)


def _pool_selectors():
    sel = np.zeros((4, _SB * DEC_SEQ, _SB * 16), np.float32)
    for gi, w in enumerate(POOL_WINDOWS):
        for s in range(_SB):
            for t in range(DEC_SEQ):
                for i in range(POOL_HIST):
                    if i >= POOL_HIST + 1 + t - w:
                        sel[gi, s * DEC_SEQ + t, s * 16 + i] = 1.0
    return sel


def _conv_selectors():
    sel = np.zeros((3, _SB * DEC_SEQ, _SB * 8), np.float32)
    for j in range(3):
        for s in range(_SB):
            for t in range(DEC_SEQ):
                if t + j < 3:
                    sel[j, s * DEC_SEQ + t, s * 8 + t + j] = 1.0
    return sel


def kernel(x_prompt, x_sample, state_pool_l1, state_gla_l2, state_ssm_l3, state_conv_l3, p_prompt, p_sample, norm_ffn1, ffn1_gate, ffn1_up, ffn1_down, norm_mix, norm_ffn2, ffn2_gate, ffn2_up, ffn2_down, norm_ple, ple_gate, ple_proj, norm_final, gm_w_in, gm_ln, gm_w_s, gm_b_s, gm_w_out, pool_w, pool_scale, gla_w_in, gla_w_a1, gla_w_a2, gla_b_a, gla_norm, gla_w_out, ssm_w_in, ssm_conv_w, ssm_conv_b, ssm_dt_bias, ssm_a_log, ssm_d, ssm_norm, ssm_w_out):
    nbp, seq, _ = x_prompt.shape
    nbs, dseq, _ = x_sample.shape
    assert dseq == DEC_SEQ and seq % 512 == 0 and nbs % _GLA_SB == 0
    cast = lambda w: w.astype(bf16)

    w1g, w1u, w1d = ffn1_gate, ffn1_up, ffn1_down
    w2g, w2u, w2d = ffn2_gate, ffn2_up, ffn2_down
    wpg, wpp = cast(ple_gate), cast(ple_proj)
    gm_in, gm_out = cast(gm_w_in), cast(gm_w_out)
    pw = cast(pool_w)
    gla_in, gla_a1, gla_a2, gla_out = cast(gla_w_in), cast(gla_w_a1), cast(gla_w_a2), cast(gla_w_out)
    ssm_in, ssm_out = cast(ssm_w_in), cast(ssm_w_out)
    lane_pad = lambda v: jnp.pad(v, [(0, 0)] * (v.ndim - 1) + [(0, 128 - SSM_H)])
    ssm_wdt = lane_pad(cast(ssm_w_in[:, SSM_DI + SSM_CONV_DIM:]))
    n1g, n2g, npg = (n.reshape(DEPTH, 1, D) for n in (norm_ffn1, norm_ffn2, norm_ple))

    dx = jnp.repeat(ssm_d, SSM_P).reshape(1, SSM_DI)
    dtb = ssm_dt_bias.reshape(1, SSM_H)
    alog = ssm_a_log.reshape(1, SSM_H)
    cbias = ssm_conv_b.reshape(1, SSM_CONV_DIM)
    bst = jnp.transpose(gm_b_s)

    t8 = np.arange(8) % DEC_SEQ
    wj = []
    for j in range(DEC_SEQ):
        src = t8 - j
        vals = gm_w_s[:, t8, np.maximum(src, 0)]
        vals = jnp.where(jnp.asarray(src >= 0)[None, :], vals, 0.0)
        wj.append(jnp.repeat(jnp.transpose(vals), 128, axis=1))
    wj = jnp.stack(wj)
    bt = jnp.repeat(jnp.transpose(gm_b_s[:, t8]), 128, axis=1)

    pool_sel = jnp.asarray(_pool_selectors(), bf16)
    conv_sel = jnp.asarray(_conv_selectors(), bf16)

    xp = x_prompt.reshape(nbp * seq, D)
    xs = x_sample.reshape(nbs * dseq, D)
    pp = p_prompt.reshape(DEPTH, nbp * seq, D_PLE)
    ps = p_sample.reshape(DEPTH, nbs * dseq, D_PLE)
    ssm_s0 = state_ssm_l3.reshape(nbs, SSM_DI, SSM_N)

    outs = {}
    for i in range(DEPTH):
        xp, xs = _ffn(xp, xs, i, n1g, w1g, w1u, w1d)
        gmix = norm_mix[i]
        if i == 0:
            xp = _gmlp_prompt(xp, gmix, gm_in, gm_ln, gm_w_s, bst, gm_out)
            xs, chunk_v = _gmlp_sample(xs, gmix, gm_in, gm_ln, wj, bt, gm_out)
            outs["chunk_v"] = chunk_v.reshape(nbs, dseq, D)
        elif i == 1:
            xp, outs["pool_p"] = _pool_prompt(xp, gmix, pw, pool_scale, nbp)
            xs, outs["pool_s"] = _pool_sample(xs, state_pool_l1, gmix, pw, pool_scale, pool_sel, PAST_LEN)
        elif i == 2:
            xp, outs["gla_p"] = _gla_prompt(xp, gmix, gla_in, gla_a1, gla_a2, gla_b_a, gla_norm, gla_out, nbp)
            qkvr, la = _gla_pre(xs, gmix, gla_in, gla_a1, gla_a2, gla_b_a)
            o, outs["gla_s"] = _gla_sample(qkvr, la, state_gla_l2)
            xs = _gla_post(o, qkvr, xs, gla_norm, gla_out)
        else:
            xp, sn, outs["conv_p"] = _ssd_prompt(xp, gmix, ssm_in, ssm_wdt, ssm_conv_w, cbias, lane_pad(dtb),
                                                 lane_pad(alog), dx, ssm_norm, ssm_out, nbp)
            outs["ssm_p"] = sn.reshape(nbp, SSM_H, SSM_P, SSM_N)
            z, xbc, dtr = _ssd_pre(xs, gmix, ssm_in)
            y, sn, outs["conv_s"] = _ssd_sample(xbc, dtr, state_conv_l3, ssm_s0, ssm_conv_w, cbias, dtb,
                                                lane_pad(alog), dx, conv_sel)
            outs["ssm_s"] = sn.reshape(nbs, SSM_H, SSM_P, SSM_N)
            xs = _ssd_post(y, z, xs, ssm_norm, ssm_out)
        fin = norm_final if i == DEPTH - 1 else None
        xp, xs = _ffn(xp, xs, i, n2g, w2g, w2u, w2d, ple=(pp, ps, npg, wpg, wpp), final_g=fin)

    return (xp.reshape(nbp, seq, D), xs.reshape(nbs, dseq, D), outs["chunk_v"], outs["pool_p"], outs["pool_s"],
            outs["gla_p"], outs["gla_s"], outs["ssm_p"], outs["ssm_s"], outs["conv_p"], outs["conv_s"])
```

```python
import functools

import jax
import jax.numpy as jnp
import numpy as np
from jax import lax
from jax.experimental import pallas as pl
from jax.experimental.pallas import tpu as pltpu

f32 = jnp.float32
bf16 = jnp.bfloat16

D = 1024
F = 2816
FC = 256
NF = F // FC
D_PLE = 256
EPS = 1e-6
DEPTH = 4

GM_CHUNK = 128
GM_GROUPS = 8
POOL_WINDOWS = (2, 4, 8, 16)
POOL_GD = D // 4
POOL_HIST = 15

GLA_H = 4
GLA_DK = 128
GLA_DV = 256
GLA_QK = GLA_H * GLA_DK
GLA_V = GLA_H * GLA_DV
GLA_NORMALIZER = 16.0
GLA_C = 128

SSM_DI = 2048
SSM_P = 64
SSM_H = 32
SSM_G = 4
SSM_HPG = 8
SSM_N = 128
SSM_CONV_DIM = SSM_DI + 2 * SSM_G * SSM_N
SSM_C = 128
GD = SSM_HPG * SSM_P

DEC_SEQ = 4
PAST_LEN = 16384

VMEM_LIMIT = 60 * 1024 * 1024


def _dot(a, b):
    return jnp.dot(a, b, preferred_element_type=f32)


def _dot_nt(a, b):
    return lax.dot_general(a, b, (((1,), (1,)), ((), ())), preferred_element_type=f32)


def _dot_tn(a, b):
    return lax.dot_general(a, b, (((0,), (0,)), ((), ())), preferred_element_type=f32)


def _dot_hi(a, b):
    return jnp.dot(a, b, precision=lax.Precision.HIGHEST, preferred_element_type=f32)


def _split3(x):
    hi = x.astype(bf16)
    r1 = x - hi.astype(f32)
    mid = r1.astype(bf16)
    lo = (r1 - mid.astype(f32)).astype(bf16)
    return hi, mid, lo


def _sel_dot(sel, parts):
    hi, mid, lo = parts
    return _dot(sel, hi) + _dot(sel, mid) + _dot(sel, lo)


def _rms(x, g):
    return x * lax.rsqrt(jnp.mean(x * x, axis=-1, keepdims=True) + EPS) * g


def _sigmoid(x):
    return 0.5 + 0.5 * jnp.tanh(0.5 * x)


def _silu(x):
    hx = 0.5 * x
    return hx + hx * jnp.tanh(hx)


def _cspec(shape):
    nd = len(shape)
    return pl.BlockSpec(shape, lambda *_: (0,) * nd, pipeline_mode=pl.Buffered(1))


def _params(*sem):
    return pltpu.CompilerParams(dimension_semantics=tuple(sem), vmem_limit_bytes=VMEM_LIMIT)


def _iota2(shape, dim):
    return lax.broadcasted_iota(jnp.int32, shape, dim)


def _two_stage(step, bufs0, bufs1):
    par = lax.rem(pl.program_id(0), 2)

    @pl.when(par == 0)
    def _():
        step(bufs0, bufs1)

    @pl.when(par == 1)
    def _():
        step(bufs1, bufs0)


def _stage_specs(tm, nt):
    a = pl.BlockSpec((tm, D), lambda s: (jnp.minimum(s, nt - 1), 0))
    b = pl.BlockSpec((tm, D), lambda s: (jnp.clip(s - 1, 0, nt - 1), 0))
    return a, b


def _ffn_kernel(*refs, ple, final, n_prompt):
    xp_ref, xs_ref, g_ref, wg_ref, wu_ref, wd_ref = refs[:6]
    k = 6
    if ple:
        pp_ref, ps_ref, gp_ref, wpg_ref, wpp_ref = refs[k:k + 5]
        k += 5
    if final:
        gf_ref = refs[k]
        k += 1
    op_ref, os_ref, h_scr, acc_scr = refs[k:k + 4]

    def gate_up(j):
        sl = slice(j * FC, (j + 1) * FC)
        hb = h_scr[...]
        return _dot(hb, wg_ref[:, sl].astype(bf16)), _dot(hb, wu_ref[:, sl].astype(bf16))

    def tile(x_ref, p_ref, o_ref):
        x = x_ref[...]
        h_scr[...] = _rms(x, g_ref[...]).astype(bf16)
        gu = gate_up(0)
        for j in range(NF):
            a = (_silu(gu[0]) * gu[1]).astype(bf16)
            if j + 1 < NF:
                gu = gate_up(j + 1)
            contrib = _dot(a, wd_ref[j * FC:(j + 1) * FC, :].astype(bf16))
            if j == 0:
                acc_scr[...] = contrib
            else:
                acc_scr[...] += contrib
        y = x + 0.5 * acc_scr[...]
        if ple:
            h2 = _rms(y, gp_ref[...]).astype(bf16)
            gate = _sigmoid(_dot(h2, wpg_ref[...]))
            y = y + gate * _dot(p_ref[...].astype(bf16), wpp_ref[...])
        if final:
            y = _rms(y, gf_ref[...])
        o_ref[...] = y

    is_prompt = pl.program_id(0) < n_prompt

    @pl.when(is_prompt)
    def _():
        tile(xp_ref, pp_ref if ple else None, op_ref)

    @pl.when(jnp.logical_not(is_prompt))
    def _():
        tile(xs_ref, ps_ref if ple else None, os_ref)


def _lspec(shape, layer):
    nd = len(shape)
    return pl.BlockSpec((None,) + tuple(shape), lambda *_: (layer,) + (0,) * nd,
                        pipeline_mode=pl.Buffered(1))


def _ffn(xp, xs, layer, g, wg, wu, wd, ple=None, final_g=None):
    tm = xs.shape[0]
    n_prompt = xp.shape[0] // tm
    prow = lambda w: pl.BlockSpec((tm, w), lambda i: (jnp.minimum(i, n_prompt - 1), 0))
    srow = lambda w: pl.BlockSpec((tm, w), lambda i: (0, 0), pipeline_mode=pl.Buffered(1))
    in_specs = [prow(D), srow(D), _lspec((1, D), layer), _lspec((D, F), layer), _lspec((D, F), layer),
                _lspec((F, D), layer)]
    args = [xp, xs, g, wg, wu, wd]
    if ple is not None:
        pp, ps, gp, wpg, wpp = ple
        in_specs += [pl.BlockSpec((None, tm, D_PLE), lambda i: (layer, jnp.minimum(i, n_prompt - 1), 0)),
                     pl.BlockSpec((None, tm, D_PLE), lambda i: (layer, 0, 0), pipeline_mode=pl.Buffered(1)),
                     _lspec((1, D), layer), _lspec((D, D), layer), _lspec((D_PLE, D), layer)]
        args += [pp, ps, gp, wpg, wpp]
    if final_g is not None:
        in_specs.append(_cspec((1, D)))
        args.append(final_g.reshape(1, D))
    return pl.pallas_call(
        functools.partial(_ffn_kernel, ple=ple is not None, final=final_g is not None, n_prompt=n_prompt),
        out_shape=(jax.ShapeDtypeStruct(xp.shape, f32), jax.ShapeDtypeStruct(xs.shape, f32)),
        grid=(n_prompt + 1,),
        in_specs=in_specs,
        out_specs=(prow(D), pl.BlockSpec((tm, D), lambda i: (0, 0))),
        scratch_shapes=[pltpu.VMEM((tm, D), bf16), pltpu.VMEM((tm, D), f32)],
        compiler_params=_params("arbitrary"),
        name="ffn",
    )(*args)


def _gelu(x):
    c = 0.7978845608028654
    hx = 0.5 * x
    return hx + hx * jnp.tanh(x * (c + (c * 0.044715) * (x * x)))


def _gmlp_act(y, ln_ref):
    y = _gelu(y)
    u = y[:, :D]
    v = y[:, D:]
    vc = v - jnp.mean(v, axis=-1, keepdims=True)
    v = vc * lax.rsqrt(jnp.mean(vc * vc, axis=-1, keepdims=True) + EPS) * ln_ref[...]
    return u, v


def _gmlp_pre(x, g, win_ref, ln_ref):
    h = _rms(x, g).astype(bf16)
    return _gmlp_act(_dot(h, win_ref[...]), ln_ref)


def _gmlp_prompt_kernel(x_ref, g_ref, win_ref, ln_ref, ws_ref, bst_ref, wout_ref, o_ref,
                        u_scr, vb_scr, t_scr):
    tm = x_ref.shape[0]
    x = x_ref[...]
    t_scr[...] = _rms(x, g_ref[...]).astype(bf16)
    cbw = D // 2
    gelu_block = lambda c0: _gelu(_dot(t_scr[...], win_ref[:, c0:c0 + cbw]))
    v = jnp.concatenate([gelu_block(D + i * cbw) for i in range(2)], axis=1)
    vc = v - jnp.mean(v, axis=-1, keepdims=True)
    v = vc * lax.rsqrt(jnp.mean(vc * vc, axis=-1, keepdims=True) + EPS) * ln_ref[...]
    vb_scr[...] = v.astype(bf16)
    for i in range(2):
        u_scr[:, i * cbw:(i + 1) * cbw] = gelu_block(i * cbw)
    causal = _iota2((GM_CHUNK, GM_CHUNK), 0) >= _iota2((GM_CHUNK, GM_CHUNK), 1)
    for g in range(GM_GROUPS):
        wm = jnp.where(causal, ws_ref[g], 0.0).astype(bf16)
        bias = bst_ref[:, g:g + 1]
        cs = slice(g * 128, (g + 1) * 128)
        for ch in range(tm // GM_CHUNK):
            rs = slice(ch * GM_CHUNK, (ch + 1) * GM_CHUNK)
            sv = _dot(wm, vb_scr[rs, cs]) + bias
            t_scr[rs, cs] = (u_scr[rs, cs] * sv).astype(bf16)
    o_ref[...] = x + _dot(t_scr[...], wout_ref[...])


def _gmlp_prompt(x, g, win, ln, ws, bst, wout, tm=512):
    T = x.shape[0]
    row = pl.BlockSpec((tm, D), lambda i: (i, 0))
    return pl.pallas_call(
        _gmlp_prompt_kernel,
        out_shape=jax.ShapeDtypeStruct((T, D), f32),
        grid=(T // tm,),
        in_specs=[row, _cspec((1, D)), _cspec((D, 2 * D)), _cspec((1, D)),
                  _cspec((GM_GROUPS, GM_CHUNK, GM_CHUNK)), _cspec((GM_CHUNK, GM_GROUPS)),
                  _cspec((D, D))],
        out_specs=row,
        scratch_shapes=[pltpu.VMEM((tm, D), f32), pltpu.VMEM((tm, D), bf16), pltpu.VMEM((tm, D), bf16)],
        compiler_params=_params("arbitrary"),
        name="gmlp_prompt",
    )(x, g.reshape(1, D), win, ln.reshape(1, D), ws, bst, wout)


def _gmlp_sample_kernel(x_ref, g_ref, win_ref, ln_ref, wj_ref, bt_ref, wout_ref, o_ref, v_ref):
    rows = x_ref.shape[0]
    n = rows // 8
    x = x_ref[...]
    u, v = _gmlp_pre(x, g_ref[...], win_ref, ln_ref)
    v_ref[...] = v
    acc = v.reshape(n, 8, D) * wj_ref[0][None]
    for j in range(1, DEC_SEQ):
        acc = acc + pltpu.roll(v, j, axis=0).reshape(n, 8, D) * wj_ref[j][None]
    sv = (acc + bt_ref[...][None]).reshape(rows, D)
    o_ref[...] = x + _dot((u * sv).astype(bf16), wout_ref[...])


def _gmlp_sample(x, g, win, ln, wj, bt, wout):
    T = x.shape[0]
    full = pl.BlockSpec((T, D), lambda i: (0, 0))
    return pl.pallas_call(
        _gmlp_sample_kernel,
        out_shape=(jax.ShapeDtypeStruct((T, D), f32), jax.ShapeDtypeStruct((T, D), f32)),
        grid=(1,),
        in_specs=[full, _cspec((1, D)), _cspec((D, 2 * D)), _cspec((1, D)),
                  _cspec((DEC_SEQ, 8, D)), _cspec((8, D)), _cspec((D, D))],
        out_specs=(full, full),
        compiler_params=_params("arbitrary"),
        name="gmlp_sample",
    )(x, g.reshape(1, D), win, ln.reshape(1, D), wj, bt, wout)


def _pool_prompt_kernel(x_ref, g_ref, pw_ref, sc_ref, o_ref, pn_ref, ext_scr, *, nl):
    tm = x_ref.shape[0]
    l = pl.program_id(1)

    @pl.when(l == 0)
    def _():
        ext_scr[0:16, :] = jnp.zeros((16, D), f32)

    x = x_ref[...]
    ext_scr[16:16 + tm, :] = _rms(x, g_ref[...])
    pos = l * tm + _iota2((tm, 1), 0)
    for gi, w in enumerate(POOL_WINDOWS):
        cs = slice(gi * POOL_GD, (gi + 1) * POOL_GD)
        s = ext_scr[:, cs]
        k = 1
        while k < w:
            s = s + pltpu.roll(s, k, axis=0)
            k *= 2
        s = s[16:16 + tm, :]
        h = ext_scr[16:16 + tm, cs]
        cnt = jnp.minimum(pos + 1, w).astype(f32)
        diff = (s / cnt - h).astype(bf16)
        o_ref[:, cs] = x[:, cs] + _dot(diff, pw_ref[gi]) * sc_ref[:, cs]

    @pl.when(l == nl - 1)
    def _():
        pn_ref[0] = ext_scr[16 + tm - POOL_HIST:16 + tm, :]

    ext_scr[0:16, :] = ext_scr[tm:tm + 16, :]


def _pool_prompt(x, g, pw, sc, nb, tm=512):
    T = x.shape[0]
    nl = T // nb // tm
    row = pl.BlockSpec((tm, D), lambda b, l: (b * nl + l, 0))
    return pl.pallas_call(
        functools.partial(_pool_prompt_kernel, nl=nl),
        out_shape=(jax.ShapeDtypeStruct((T, D), f32), jax.ShapeDtypeStruct((nb, POOL_HIST, D), f32)),
        grid=(nb, nl),
        in_specs=[row, _cspec((1, D)), _cspec((4, POOL_GD, POOL_GD)), _cspec((1, D))],
        out_specs=(row, pl.BlockSpec((1, POOL_HIST, D), lambda b, l: (b, 0, 0))),
        scratch_shapes=[pltpu.VMEM((tm + 16, D), f32)],
        compiler_params=_params("arbitrary", "arbitrary"),
        name="pool_prompt",
    )(x, g.reshape(1, D), pw, sc.reshape(1, D))


_SB = 8


def _pool_sample_kernel(x_ref, hist_ref, g_ref, pw_ref, sc_ref, sel_ref, o_ref, pn_ref, hp_scr, *, pos0):
    rows = x_ref.shape[0]
    nseq = hist_ref.shape[0]
    x = x_ref[...]
    h = _rms(x, g_ref[...])
    hp_scr[...] = jnp.zeros_like(hp_scr)
    for s in range(nseq):
        hp_scr[16 * s:16 * s + POOL_HIST, :] = hist_ref[s]
        pn_ref[s, 0:POOL_HIST - DEC_SEQ, :] = hist_ref[s, DEC_SEQ:POOL_HIST, :]
        pn_ref[s, POOL_HIST - DEC_SEQ:POOL_HIST, :] = h[DEC_SEQ * s:DEC_SEQ * (s + 1), :]
    t = _iota2((rows, 1), 0) & (DEC_SEQ - 1)
    parts = _split3(hp_scr[...])
    for gi, w in enumerate(POOL_WINDOWS):
        cs = slice(gi * POOL_GD, (gi + 1) * POOL_GD)
        hc = h[:, cs]
        s = hc
        for j in range(1, min(w, DEC_SEQ)):
            s = s + jnp.where(t >= j, pltpu.roll(hc, j, axis=0), 0.0)
        s = s + _sel_dot(sel_ref[gi], tuple(p[:, cs] for p in parts))
        cnt = jnp.minimum(pos0 + t + 1, w).astype(f32)
        diff = (s / cnt - hc).astype(bf16)
        o_ref[:, cs] = x[:, cs] + _dot(diff, pw_ref[gi]) * sc_ref[:, cs]


def _pool_sample(x, hist, g, pw, sc, sel, pos0):
    T = x.shape[0]
    rb = _SB * DEC_SEQ
    row = pl.BlockSpec((rb, D), lambda i: (i, 0))
    seq = pl.BlockSpec((_SB, POOL_HIST, D), lambda i: (i, 0, 0))
    return pl.pallas_call(
        functools.partial(_pool_sample_kernel, pos0=pos0),
        out_shape=(jax.ShapeDtypeStruct((T, D), f32), jax.ShapeDtypeStruct(hist.shape, f32)),
        grid=(T // rb,),
        in_specs=[row, seq, _cspec((1, D)), _cspec((4, POOL_GD, POOL_GD)), _cspec((1, D)),
                  _cspec((4, rb, _SB * 16))],
        out_specs=(row, seq),
        scratch_shapes=[pltpu.VMEM((_SB * 16, D), f32)],
        compiler_params=_params("arbitrary"),
        name="pool_sample",
    )(x, hist, g.reshape(1, D), pw, sc.reshape(1, D), sel)


def _cumsum_rows(x, seg=None):
    n = x.shape[0]
    seg = n if seg is None else seg
    pos = _iota2((n, 1), 0) & (seg - 1)
    s = 1
    while s < seg:
        x = x + jnp.where(pos >= s, pltpu.roll(x, s, axis=0), 0.0)
        s *= 2
    return x


def _gla_pre_body(x, g_ref, win_ref, wa1_ref, wa2_ref, ba_ref, qkvr_ref, la_ref):
    h = _rms(x, g_ref[...]).astype(bf16)
    qkvr_ref[...] = _dot(h, win_ref[...])
    a1 = _dot(h, wa1_ref[...]).astype(bf16)
    z = _dot(a1, wa2_ref[...]) + ba_ref[...]
    la_ref[...] = jax.nn.log_sigmoid(z) / GLA_NORMALIZER


def _gla_post_body(o, r, x, gn_ref, wout_ref):
    parts = []
    for hd in range(GLA_H):
        vs = slice(hd * GLA_DV, (hd + 1) * GLA_DV)
        parts.append(_rms(o[:, vs], gn_ref[:, vs]))
    on = jnp.concatenate(parts, axis=1)
    gated = (on * _silu(r)).astype(bf16)
    return x + _dot(gated, wout_ref[...])


def _gla_pre_kernel(x_ref, g_ref, win_ref, wa1_ref, wa2_ref, ba_ref, qkvr_ref, la_ref):
    _gla_pre_body(x_ref[...], g_ref, win_ref, wa1_ref, wa2_ref, ba_ref, qkvr_ref, la_ref)


def _gla_pre(x, g, win, wa1, wa2, ba, tm=512):
    T = x.shape[0]
    W = 2 * GLA_QK + 2 * GLA_V
    return pl.pallas_call(
        _gla_pre_kernel,
        out_shape=(jax.ShapeDtypeStruct((T, W), f32), jax.ShapeDtypeStruct((T, GLA_QK), f32)),
        grid=(T // tm,),
        in_specs=[pl.BlockSpec((tm, D), lambda i: (i, 0)), _cspec((1, D)), _cspec((D, W)),
                  _cspec((D, 16)), _cspec((16, GLA_QK)), _cspec((1, GLA_QK))],
        out_specs=(pl.BlockSpec((tm, W), lambda i: (i, 0)), pl.BlockSpec((tm, GLA_QK), lambda i: (i, 0))),
        compiler_params=_params("arbitrary"),
        name="gla_pre",
    )(x, g.reshape(1, D), win, wa1, wa2, ba.reshape(1, GLA_QK))


def _gla_chunk(q, k, v, b, bl, anc, mask, segs, C, halves=False):
    qs = q * (GLA_DK ** -0.5)
    q_dec = (qs * jnp.exp(b)).astype(bf16)
    k_end = k * jnp.exp(bl - b)
    q_mid = (qs * jnp.exp(b - anc)).astype(bf16)
    k_mid = (k * jnp.exp(anc - b)).astype(bf16)
    ebl = jnp.exp(bl)
    vb = v.astype(bf16)
    rid = _iota2((C, 1), 0)
    if halves:
        hc = C // 2
        edge = b[hc - 1:hc, :]
        q_x = jnp.where(rid >= hc, qs * jnp.exp(b - edge), 0.0).astype(bf16)
        k_x = jnp.where(rid < hc, k * jnp.exp(edge - b), 0.0).astype(bf16)
        mask = mask & ((_iota2((C, C), 0) >= hc) == (_iota2((C, C), 1) >= hc))
    outs = []
    for hd in range(GLA_H):
        ds_ = slice(hd * GLA_DK, (hd + 1) * GLA_DK)
        vs = slice(hd * GLA_DV, (hd + 1) * GLA_DV)
        sc = jnp.where(mask, _dot_nt(q_mid[:, ds_], k_mid[:, ds_]), 0.0)
        if halves:
            sc = sc + _dot_nt(q_x[:, ds_], k_x[:, ds_])
        o = _dot(sc.astype(bf16), vb[:, vs])
        o_int = None
        for (lo, hi, s_in, s_out) in segs:
            oi = _dot(q_dec[:, ds_], s_in[hd].astype(bf16))
            o_int = oi if o_int is None else jnp.where(rid >= lo, oi, o_int)
        outs.append(o + o_int)
        for (lo, hi, s_in, s_out) in segs:
            if ebl.shape[0] == 1:
                dec = jnp.transpose(jnp.broadcast_to(ebl[:, ds_], (8, GLA_DK)))[:, 0:1]
            else:
                r0 = max(hi - 8, 0)
                dec = jnp.transpose(ebl[r0:r0 + 8, ds_])[:, hi - 1 - r0:hi - r0]
            ke = k_end[:, ds_]
            if len(segs) > 1:
                ke = jnp.where((rid >= lo) & (rid < hi), ke, 0.0)
            s_out[hd] = s_in[hd] * dec + _dot_tn(ke.astype(bf16), vb[:, vs])
    return jnp.concatenate(outs, axis=1)


def _gla_prompt_kernel(xa_ref, xb_ref, g_ref, win_ref, wa1_ref, wa2_ref, ba_ref, gn_ref, wout_ref,
                       out_ref, sn_ref, q0, l0, q1, l1, o_scr, s_scr, *, nl):
    tm = xa_ref.shape[0]
    C = GLA_C
    s = pl.program_id(0)
    tpos = lax.rem(s - 1, nl)

    @pl.when(s == 0)
    def _():
        q1[...] = jnp.zeros_like(q1)
        l1[...] = jnp.zeros_like(l1)

    @pl.when((s == 0) | (tpos == 0))
    def _():
        s_scr[...] = jnp.zeros_like(s_scr)

    def step(bw, br):
        qw, lw = bw
        qr, lr = br
        _gla_pre_body(xa_ref[...], g_ref, win_ref, wa1_ref, wa2_ref, ba_ref, qw, lw)
        tri = _iota2((C, C), 0) >= _iota2((C, C), 1)
        for c in range(tm // C):
            rs = slice(c * C, (c + 1) * C)
            b = _cumsum_rows(lr[rs, :])
            bl = b[C - 1:C, :]
            anc = jnp.where(_iota2((C, 1), 0) < C // 2, b[C // 4 - 1:C // 4, :], b[3 * C // 4 - 1:3 * C // 4, :])
            q = qr[rs, 0:GLA_QK]
            k = qr[rs, GLA_QK:2 * GLA_QK]
            v = qr[rs, 2 * GLA_QK:2 * GLA_QK + GLA_V]
            o_scr[rs, :] = _gla_chunk(q, k, v, b, bl, anc, tri, [(0, C, s_scr, s_scr)], C, halves=True)
        out_ref[...] = _gla_post_body(o_scr[...], qr[:, 2 * GLA_QK + GLA_V:], xb_ref[...], gn_ref, wout_ref)

    _two_stage(step, (q0, l0), (q1, l1))

    @pl.when(tpos == nl - 1)
    def _():
        sn_ref[0] = s_scr[...]


def _gla_prompt(x, g, win, wa1, wa2, ba, gn, wout, nb, tm=256):
    T = x.shape[0]
    nt = T // tm
    nl = nt // nb
    W = 2 * GLA_QK + 2 * GLA_V
    a, b = _stage_specs(tm, nt)
    return pl.pallas_call(
        functools.partial(_gla_prompt_kernel, nl=nl),
        out_shape=(jax.ShapeDtypeStruct((T, D), f32),
                   jax.ShapeDtypeStruct((nb, GLA_H, GLA_DK, GLA_DV), f32)),
        grid=(nt + 1,),
        in_specs=[a, b, _cspec((1, D)), _cspec((D, W)), _cspec((D, 16)), _cspec((16, GLA_QK)),
                  _cspec((1, GLA_QK)), _cspec((1, GLA_V)), _cspec((GLA_V, D))],
        out_specs=(b, pl.BlockSpec((1, GLA_H, GLA_DK, GLA_DV), lambda s: (jnp.maximum(s - 1, 0) // nl, 0, 0, 0))),
        scratch_shapes=[pltpu.VMEM((tm, W), f32), pltpu.VMEM((tm, GLA_QK), f32),
                        pltpu.VMEM((tm, W), f32), pltpu.VMEM((tm, GLA_QK), f32),
                        pltpu.VMEM((tm, GLA_V), f32), pltpu.VMEM((GLA_H, GLA_DK, GLA_DV), f32)],
        compiler_params=_params("arbitrary"),
        name="gla_prompt",
    )(x, x, g.reshape(1, D), win, wa1, wa2, ba.reshape(1, GLA_QK), gn.reshape(1, GLA_V), wout)


_GLA_SB = 16


def _gla_sample_kernel(qkv_ref, la_ref, s0_ref, o_ref, sn_ref, b_scr, bl_scr):
    rows = qkv_ref.shape[0]
    r = _iota2((rows, rows), 0)
    c = _iota2((rows, rows), 1)
    same = (r >> 2) == (c >> 2)
    la = la_ref[...]
    b_scr[...] = _dot_hi((same & (r >= c)).astype(f32), la)
    bl_scr[...] = _dot_hi(same.astype(f32), la)
    r8 = _iota2((8, 8), 0)
    c8 = _iota2((8, 8), 1)
    mask8 = ((r8 >> 2) == (c8 >> 2)) & (r8 >= c8)

    def body(p, carry):
        rs = pl.ds(pl.multiple_of(p * 8, 8), 8)
        q = qkv_ref[rs, 0:GLA_QK]
        k = qkv_ref[rs, GLA_QK:2 * GLA_QK]
        v = qkv_ref[rs, 2 * GLA_QK:2 * GLA_QK + GLA_V]
        segs = [(0, 4, s0_ref.at[2 * p], sn_ref.at[2 * p]),
                (4, 8, s0_ref.at[2 * p + 1], sn_ref.at[2 * p + 1])]
        o_ref[rs, :] = _gla_chunk(q, k, v, b_scr[rs, :], bl_scr[rs, :], 0.0, mask8, segs, 8)
        return carry

    lax.fori_loop(0, rows // 8, body, 0)


def _gla_sample(qkvr, la, s0):
    T = qkvr.shape[0]
    nb = s0.shape[0]
    rb = _GLA_SB * DEC_SEQ
    W = 2 * GLA_QK + GLA_V
    st = pl.BlockSpec((_GLA_SB, GLA_H, GLA_DK, GLA_DV), lambda i: (i, 0, 0, 0))
    return pl.pallas_call(
        _gla_sample_kernel,
        out_shape=(jax.ShapeDtypeStruct((T, GLA_V), f32),
                   jax.ShapeDtypeStruct((nb, GLA_H, GLA_DK, GLA_DV), f32)),
        grid=(T // rb,),
        in_specs=[pl.BlockSpec((rb, W), lambda i: (i, 0)), pl.BlockSpec((rb, GLA_QK), lambda i: (i, 0)), st],
        out_specs=(pl.BlockSpec((rb, GLA_V), lambda i: (i, 0)), st),
        scratch_shapes=[pltpu.VMEM((rb, GLA_QK), f32), pltpu.VMEM((rb, GLA_QK), f32)],
        compiler_params=_params("arbitrary"),
        name="gla_sample",
    )(qkvr, la, s0)


def _gla_post_kernel(o_ref, r_ref, x_ref, gn_ref, wout_ref, out_ref):
    out_ref[...] = _gla_post_body(o_ref[...], r_ref[...], x_ref[...], gn_ref, wout_ref)


def _gla_post(o, qkvr, x, gn, wout, tm=512):
    T = x.shape[0]
    row = pl.BlockSpec((tm, D), lambda i: (i, 0))
    return pl.pallas_call(
        _gla_post_kernel,
        out_shape=jax.ShapeDtypeStruct((T, D), f32),
        grid=(T // tm,),
        in_specs=[row, pl.BlockSpec((tm, GLA_V), lambda i: (i, 2)), row, _cspec((1, GLA_V)),
                  _cspec((GLA_V, D))],
        out_specs=row,
        compiler_params=_params("arbitrary"),
        name="gla_post",
    )(o, qkvr, x, gn.reshape(1, GLA_V), wout)


def _ssd_pre_kernel(x_ref, g_ref, win_ref, z_ref, xbc_ref, dt_ref):
    h = _rms(x_ref[...], g_ref[...]).astype(bf16)
    z_ref[...] = _dot(h, win_ref[:, 0:SSM_DI])
    xbc_ref[...] = _dot(h, win_ref[:, SSM_DI:SSM_DI + SSM_CONV_DIM])
    dt_ref[...] = _dot(h, win_ref[:, SSM_DI + SSM_CONV_DIM:])


def _ssd_pre(x, g, win, tm=512):
    T = x.shape[0]
    return pl.pallas_call(
        _ssd_pre_kernel,
        out_shape=(jax.ShapeDtypeStruct((T, SSM_DI), f32), jax.ShapeDtypeStruct((T, SSM_CONV_DIM), f32),
                   jax.ShapeDtypeStruct((T, SSM_H), f32)),
        grid=(T // tm,),
        in_specs=[pl.BlockSpec((tm, D), lambda i: (i, 0)), _cspec((1, D)), _cspec((D, win.shape[1]))],
        out_specs=(pl.BlockSpec((tm, SSM_DI), lambda i: (i, 0)),
                   pl.BlockSpec((tm, SSM_CONV_DIM), lambda i: (i, 0)),
                   pl.BlockSpec((tm, SSM_H), lambda i: (i, 0))),
        compiler_params=_params("arbitrary"),
        name="ssd_pre",
    )(x, g.reshape(1, D), win)


def _conv_silu(terms, cw_ref, cb_ref):
    acc = terms[0] * cw_ref[0:1, :]
    for j in range(1, 4):
        acc = acc + terms[j] * cw_ref[j:j + 1, :]
    return _silu(cb_ref[...] + acc)


def _ssd_chunk(act_ref, rs, dtc, a_row, dx_ref, y_ref, mask, segs, C):
    rid = _iota2((C, 1), 0)
    cum = _cumsum_rows(dtc * a_row, segs[0][1] - segs[0][0])
    cum_last = cum[segs[-1][1] - 1:segs[-1][1], :]
    for (lo, hi, _, _) in segs[-2::-1]:
        cum_last = jnp.where(rid < hi, cum[hi - 1:hi, :], cum_last)
    e_cum = jnp.exp(cum)
    wend = jnp.exp(cum_last - cum) * dtc
    cum_t = jnp.transpose(cum)
    dt_t = jnp.transpose(dtc)
    dec_t = jnp.exp(cum_t)
    lane_lo = _iota2((C, 2 * SSM_P), 1) < SSM_P
    for g in range(SSM_G):
        gs = slice(g * GD, (g + 1) * GD)
        bm = act_ref[rs, SSM_DI + g * SSM_N:SSM_DI + (g + 1) * SSM_N].astype(bf16)
        cm = act_ref[rs, SSM_DI + SSM_G * SSM_N + g * SSM_N:SSM_DI + SSM_G * SSM_N + (g + 1) * SSM_N].astype(bf16)
        cb = _dot_nt(cm, bm)
        y_int = None
        for (lo, hi, s_in, s_out) in segs:
            yi = _dot_nt(cm, s_in[gs, :].astype(bf16))
            y_int = yi if y_int is None else jnp.where(rid >= lo, yi, y_int)
        xw = []
        for pr in range(SSM_HPG // 2):
            h0 = g * SSM_HPG + 2 * pr
            cols = slice(g * GD + 2 * pr * SSM_P, g * GD + (2 * pr + 2) * SSM_P)
            xs = act_ref[rs, cols]
            e_pair = jnp.where(lane_lo, e_cum[:, h0:h0 + 1], e_cum[:, h0 + 1:h0 + 2])
            w_pair = jnp.where(lane_lo, wend[:, h0:h0 + 1], wend[:, h0 + 1:h0 + 2])
            acc = y_int[:, 2 * pr * SSM_P:(2 * pr + 2) * SSM_P] * e_pair + xs * dx_ref[:, cols]
            xp = xs.astype(bf16)
            for hh, keep in ((h0, lane_lo), (h0 + 1, ~lane_lo)):
                seg = cum[:, hh:hh + 1] - cum_t[hh:hh + 1, :]
                mix = cb * jnp.exp(jnp.where(mask, seg, -jnp.inf)) * dt_t[hh:hh + 1, :]
                acc = acc + _dot(mix.astype(bf16), jnp.where(keep, xp, jnp.zeros_like(xp)))
            y_ref[rs, cols] = acc
            xw.append(xs * w_pair)
        xw = jnp.concatenate(xw, axis=1)
        for (lo, hi, s_in, s_out) in segs:
            xws = xw if len(segs) == 1 else jnp.where((rid >= lo) & (rid < hi), xw, 0.0)
            upd = _dot_tn(xws.astype(bf16), bm)
            for r in range(SSM_HPG):
                hh = g * SSM_HPG + r
                hs = slice(hh * SSM_P, (hh + 1) * SSM_P)
                s_out[hs, :] = s_in[hs, :] * dec_t[hh:hh + 1, hi - 1:hi] + upd[r * SSM_P:(r + 1) * SSM_P, :]


def _ssd_post_body(y, z, x, gn_ref, wout_ref):
    y = y * _silu(z)
    parts = []
    for g in range(SSM_G):
        gs = slice(g * GD, (g + 1) * GD)
        parts.append(_rms(y[:, gs], gn_ref[:, gs]))
    yn = jnp.concatenate(parts, axis=1).astype(bf16)
    return x + _dot(yn, wout_ref[...])


def _ssd_prompt_kernel(xa_ref, xb_ref, g_ref, win_ref, wdt_ref, cw_ref, cb_ref, dtb_ref, alog_ref, dx_ref, gn_ref,
                       wout_ref, out_ref, sn_ref, cn_ref, z0, e0, d0, z1, e1, d1, carry_scr, act_scr, y_scr, s_scr,
                       *, nl):
    tm = xa_ref.shape[0]
    C = SSM_C
    s = pl.program_id(0)
    tpos = lax.rem(s - 1, nl)

    @pl.when(s == 0)
    def _():
        z1[...] = jnp.zeros_like(z1)
        e1[...] = jnp.zeros_like(e1)
        d1[...] = jnp.zeros_like(d1)

    @pl.when((s == 0) | (tpos == 0))
    def _():
        carry_scr[...] = jnp.zeros_like(carry_scr)
        s_scr[...] = jnp.zeros_like(s_scr)

    def step(bw, br):
        zw, ew, dw = bw
        zr, er, dr = br
        h = _rms(xa_ref[...], g_ref[...]).astype(bf16)
        zw[...] = _dot(h, win_ref[:, 0:SSM_DI])
        ew[8:8 + tm, :] = _dot(h, win_ref[:, SSM_DI:SSM_DI + SSM_CONV_DIM])
        dw[...] = _dot(h, wdt_ref[...])
        er[0:8, :] = carry_scr[...]
        e = er[...]
        acc = e * cw_ref[0:1, :]
        for j in range(1, 4):
            acc = pltpu.roll(acc, 1, axis=0) + e * cw_ref[j:j + 1, :]
        act_scr[...] = _silu(cb_ref[...] + acc[8:8 + tm, :])
        carry_scr[...] = er[tm:tm + 8, :]
        a_row = -jnp.exp(alog_ref[...])
        tri = _iota2((C, C), 0) >= _iota2((C, C), 1)
        for c in range(tm // C):
            rs = slice(c * C, (c + 1) * C)
            dtc = jax.nn.softplus(dr[rs, :] + dtb_ref[...])
            _ssd_chunk(act_scr, rs, dtc, a_row, dx_ref, y_scr, tri, [(0, C, s_scr, s_scr)], C)
        out_ref[...] = _ssd_post_body(y_scr[...], zr[...], xb_ref[...], gn_ref, wout_ref)

        @pl.when(tpos == nl - 1)
        def _():
            cn_ref[0] = er[8 + tm - 3:8 + tm, :]

    _two_stage(step, (z0, e0, d0), (z1, e1, d1))

    @pl.when(tpos == nl - 1)
    def _():
        sn_ref[0] = s_scr[...]


def _ssd_prompt(x, g, win, wdt, cw, cb, dtb, alog, dx, gn, wout, nb, tm=256):
    T = x.shape[0]
    nt = T // tm
    nl = nt // nb
    WIN = win.shape[1]
    a, b = _stage_specs(tm, nt)
    seq = lambda s: (jnp.maximum(s - 1, 0) // nl, 0, 0)
    bufs = [pltpu.VMEM((tm, SSM_DI), f32), pltpu.VMEM((tm + 8, SSM_CONV_DIM), f32), pltpu.VMEM((tm, 128), f32)]
    return pl.pallas_call(
        functools.partial(_ssd_prompt_kernel, nl=nl),
        out_shape=(jax.ShapeDtypeStruct((T, D), f32),
                   jax.ShapeDtypeStruct((nb, SSM_DI, SSM_N), f32),
                   jax.ShapeDtypeStruct((nb, 3, SSM_CONV_DIM), f32)),
        grid=(nt + 1,),
        in_specs=[a, b, _cspec((1, D)), _cspec((D, WIN)), _cspec((D, 128)),
                  _cspec((4, SSM_CONV_DIM)), _cspec((1, SSM_CONV_DIM)), _cspec((1, 128)),
                  _cspec((1, 128)), _cspec((1, SSM_DI)), _cspec((1, SSM_DI)), _cspec((SSM_DI, D))],
        out_specs=(b, pl.BlockSpec((1, SSM_DI, SSM_N), seq), pl.BlockSpec((1, 3, SSM_CONV_DIM), seq)),
        scratch_shapes=bufs + bufs + [pltpu.VMEM((8, SSM_CONV_DIM), f32), pltpu.VMEM((tm, SSM_CONV_DIM), f32),
                                      pltpu.VMEM((tm, SSM_DI), f32), pltpu.VMEM((SSM_DI, SSM_N), f32)],
        compiler_params=_params("arbitrary"),
        name="ssd_prompt",
    )(x, x, g.reshape(1, D), win, wdt, cw, cb, dtb, alog, dx, gn.reshape(1, SSM_DI), wout)


def _ssd_sample_kernel(xbc_ref, dtr_ref, cs_ref, s0_ref, cw_ref, cb_ref, dtb_ref, alog_ref, dx_ref,
                       sel_ref, y_ref, sn_ref, cn_ref, act_scr, dt_scr, cp_scr):
    rows = xbc_ref.shape[0]
    xb = xbc_ref[...]
    cp_scr[...] = jnp.zeros_like(cp_scr)
    for s in range(cs_ref.shape[0]):
        cp_scr[8 * s:8 * s + 3, :] = cs_ref[s]
        cn_ref[s] = xb[DEC_SEQ * s + 1:DEC_SEQ * (s + 1), :]
    t = _iota2((rows, 1), 0) & (DEC_SEQ - 1)
    parts = _split3(cp_scr[...])
    terms = []
    for j in range(4):
        sh = 3 - j
        term = xb if sh == 0 else jnp.where(t >= sh, pltpu.roll(xb, sh, axis=0), 0.0)
        if j < 3:
            term = term + _sel_dot(sel_ref[j], parts)
        terms.append(term)
    act_scr[...] = _conv_silu(terms, cw_ref, cb_ref)
    dt_scr[...] = jnp.zeros_like(dt_scr)
    dt_scr[:, 0:SSM_H] = jax.nn.softplus(dtr_ref[...] + dtb_ref[...])
    a_row = -jnp.exp(alog_ref[...])
    r8 = _iota2((8, 8), 0)
    c8 = _iota2((8, 8), 1)
    mask8 = ((r8 >> 2) == (c8 >> 2)) & (r8 >= c8)

    def body(p, carry):
        rs = pl.ds(pl.multiple_of(p * 8, 8), 8)
        segs = [(0, 4, s0_ref.at[2 * p], sn_ref.at[2 * p]),
                (4, 8, s0_ref.at[2 * p + 1], sn_ref.at[2 * p + 1])]
        _ssd_chunk(act_scr, rs, dt_scr[rs, :], a_row, dx_ref, y_ref, mask8, segs, 8)
        return carry

    lax.fori_loop(0, rows // 8, body, 0)


def _ssd_sample(xbc, dtr, cs, s0, cw, cb, dtb, alog, dx, sel):
    T = xbc.shape[0]
    nb = s0.shape[0]
    rb = _SB * DEC_SEQ
    st = pl.BlockSpec((_SB, SSM_DI, SSM_N), lambda i: (i, 0, 0))
    cst = pl.BlockSpec((_SB, 3, SSM_CONV_DIM), lambda i: (i, 0, 0))
    return pl.pallas_call(
        _ssd_sample_kernel,
        out_shape=(jax.ShapeDtypeStruct((T, SSM_DI), f32), jax.ShapeDtypeStruct((nb, SSM_DI, SSM_N), f32),
                   jax.ShapeDtypeStruct(cs.shape, f32)),
        grid=(T // rb,),
        in_specs=[pl.BlockSpec((rb, SSM_CONV_DIM), lambda i: (i, 0)), pl.BlockSpec((rb, SSM_H), lambda i: (i, 0)),
                  cst, st,
                  _cspec((4, SSM_CONV_DIM)), _cspec((1, SSM_CONV_DIM)), _cspec((1, SSM_H)),
                  _cspec((1, 128)), _cspec((1, SSM_DI)), _cspec((3, rb, _SB * 8))],
        out_specs=(pl.BlockSpec((rb, SSM_DI), lambda i: (i, 0)), st, cst),
        scratch_shapes=[pltpu.VMEM((rb, SSM_CONV_DIM), f32), pltpu.VMEM((rb, 128), f32),
                        pltpu.VMEM((_SB * 8, SSM_CONV_DIM), f32)],
        compiler_params=_params("arbitrary"),
        name="ssd_sample",
    )(xbc, dtr, cs, s0, cw, cb, dtb, alog, dx, sel)


def _ssd_post_kernel(y_ref, z_ref, x_ref, gn_ref, wout_ref, out_ref):
    out_ref[...] = _ssd_post_body(y_ref[...], z_ref[...], x_ref[...], gn_ref, wout_ref)


def _ssd_post(y, z, x, gn, wout, tm=512):
    T = x.shape[0]
    row = pl.BlockSpec((tm, D), lambda i: (i, 0))
    wide = pl.BlockSpec((tm, SSM_DI), lambda i: (i, 0))
    return pl.pallas_call(
        _ssd_post_kernel,
        out_shape=jax.ShapeDtypeStruct((T, D), f32),
        grid=(T // tm,),
        in_specs=[wide, wide, row, _cspec((1, SSM_DI)), _cspec((SSM_DI, D))],
        out_specs=row,
        compiler_params=_params("arbitrary"),
        name="ssd_post",
    )(y, z, x, gn.reshape(1, SSM_DI), wout)


def _pool_selectors():
    sel = np.zeros((4, _SB * DEC_SEQ, _SB * 16), np.float32)
    for gi, w in enumerate(POOL_WINDOWS):
        for s in range(_SB):
            for t in range(DEC_SEQ):
                for i in range(POOL_HIST):
                    if i >= POOL_HIST + 1 + t - w:
                        sel[gi, s * DEC_SEQ + t, s * 16 + i] = 1.0
    return sel


def _conv_selectors():
    sel = np.zeros((3, _SB * DEC_SEQ, _SB * 8), np.float32)
    for j in range(3):
        for s in range(_SB):
            for t in range(DEC_SEQ):
                if t + j < 3:
                    sel[j, s * DEC_SEQ + t, s * 8 + t + j] = 1.0
    return sel


def kernel(x_prompt, x_sample, state_pool_l1, state_gla_l2, state_ssm_l3, state_conv_l3, p_prompt, p_sample, norm_ffn1, ffn1_gate, ffn1_up, ffn1_down, norm_mix, norm_ffn2, ffn2_gate, ffn2_up, ffn2_down, norm_ple, ple_gate, ple_proj, norm_final, gm_w_in, gm_ln, gm_w_s, gm_b_s, gm_w_out, pool_w, pool_scale, gla_w_in, gla_w_a1, gla_w_a2, gla_b_a, gla_norm, gla_w_out, ssm_w_in, ssm_conv_w, ssm_conv_b, ssm_dt_bias, ssm_a_log, ssm_d, ssm_norm, ssm_w_out):
    nbp, seq, _ = x_prompt.shape
    nbs, dseq, _ = x_sample.shape
    assert dseq == DEC_SEQ and seq % 512 == 0 and nbs % _GLA_SB == 0
    cast = lambda w: w.astype(bf16)

    w1g, w1u, w1d = ffn1_gate, ffn1_up, ffn1_down
    w2g, w2u, w2d = ffn2_gate, ffn2_up, ffn2_down
    wpg, wpp = cast(ple_gate), cast(ple_proj)
    gm_in, gm_out = cast(gm_w_in), cast(gm_w_out)
    pw = cast(pool_w)
    gla_in, gla_a1, gla_a2, gla_out = cast(gla_w_in), cast(gla_w_a1), cast(gla_w_a2), cast(gla_w_out)
    ssm_in, ssm_out = cast(ssm_w_in), cast(ssm_w_out)
    lane_pad = lambda v: jnp.pad(v, [(0, 0)] * (v.ndim - 1) + [(0, 128 - SSM_H)])
    ssm_wdt = lane_pad(cast(ssm_w_in[:, SSM_DI + SSM_CONV_DIM:]))
    n1g, n2g, npg = (n.reshape(DEPTH, 1, D) for n in (norm_ffn1, norm_ffn2, norm_ple))

    dx = jnp.repeat(ssm_d, SSM_P).reshape(1, SSM_DI)
    dtb = ssm_dt_bias.reshape(1, SSM_H)
    alog = ssm_a_log.reshape(1, SSM_H)
    cbias = ssm_conv_b.reshape(1, SSM_CONV_DIM)
    bst = jnp.transpose(gm_b_s)

    t8 = np.arange(8) % DEC_SEQ
    wj = []
    for j in range(DEC_SEQ):
        src = t8 - j
        vals = gm_w_s[:, t8, np.maximum(src, 0)]
        vals = jnp.where(jnp.asarray(src >= 0)[None, :], vals, 0.0)
        wj.append(jnp.repeat(jnp.transpose(vals), 128, axis=1))
    wj = jnp.stack(wj)
    bt = jnp.repeat(jnp.transpose(gm_b_s[:, t8]), 128, axis=1)

    pool_sel = jnp.asarray(_pool_selectors(), bf16)
    conv_sel = jnp.asarray(_conv_selectors(), bf16)

    xp = x_prompt.reshape(nbp * seq, D)
    xs = x_sample.reshape(nbs * dseq, D)
    pp = p_prompt.reshape(DEPTH, nbp * seq, D_PLE)
    ps = p_sample.reshape(DEPTH, nbs * dseq, D_PLE)
    ssm_s0 = state_ssm_l3.reshape(nbs, SSM_DI, SSM_N)

    outs = {}
    for i in range(DEPTH):
        xp, xs = _ffn(xp, xs, i, n1g, w1g, w1u, w1d)
        gmix = norm_mix[i]
        if i == 0:
            xp = _gmlp_prompt(xp, gmix, gm_in, gm_ln, gm_w_s, bst, gm_out)
            xs, chunk_v = _gmlp_sample(xs, gmix, gm_in, gm_ln, wj, bt, gm_out)
            outs["chunk_v"] = chunk_v.reshape(nbs, dseq, D)
        elif i == 1:
            xp, outs["pool_p"] = _pool_prompt(xp, gmix, pw, pool_scale, nbp)
            xs, outs["pool_s"] = _pool_sample(xs, state_pool_l1, gmix, pw, pool_scale, pool_sel, PAST_LEN)
        elif i == 2:
            xp, outs["gla_p"] = _gla_prompt(xp, gmix, gla_in, gla_a1, gla_a2, gla_b_a, gla_norm, gla_out, nbp)
            qkvr, la = _gla_pre(xs, gmix, gla_in, gla_a1, gla_a2, gla_b_a)
            o, outs["gla_s"] = _gla_sample(qkvr, la, state_gla_l2)
            xs = _gla_post(o, qkvr, xs, gla_norm, gla_out)
        else:
            xp, sn, outs["conv_p"] = _ssd_prompt(xp, gmix, ssm_in, ssm_wdt, ssm_conv_w, cbias, lane_pad(dtb),
                                                 lane_pad(alog), dx, ssm_norm, ssm_out, nbp)
            outs["ssm_p"] = sn.reshape(nbp, SSM_H, SSM_P, SSM_N)
            z, xbc, dtr = _ssd_pre(xs, gmix, ssm_in)
            y, sn, outs["conv_s"] = _ssd_sample(xbc, dtr, state_conv_l3, ssm_s0, ssm_conv_w, cbias, dtb,
                                                lane_pad(alog), dx, conv_sel)
            outs["ssm_s"] = sn.reshape(nbs, SSM_H, SSM_P, SSM_N)
            xs = _ssd_post(y, z, xs, ssm_norm, ssm_out)
        fin = norm_final if i == DEPTH - 1 else None
        xp, xs = _ffn(xp, xs, i, n2g, w2g, w2u, w2d, ple=(pp, ps, npg, wpg, wpp), final_g=fin)

    return (xp.reshape(nbp, seq, D), xs.reshape(nbs, dseq, D), outs["chunk_v"], outs["pool_p"], outs["pool_s"],
            outs["gla_p"], outs["gla_s"], outs["ssm_p"], outs["ssm_s"], outs["conv_p"], outs["conv_s"])
```

```python
import functools

import jax
import jax.numpy as jnp
import numpy as np
from jax import lax
from jax.experimental import pallas as pl
from jax.experimental.pallas import tpu as pltpu

f32 = jnp.float32
bf16 = jnp.bfloat16

D = 1024
F = 2816
FC = 256
NF = F // FC
D_PLE = 256
EPS = 1e-6
DEPTH = 4

GM_CHUNK = 128
GM_GROUPS = 8
POOL_WINDOWS = (2, 4, 8, 16)
POOL_GD = D // 4
POOL_HIST = 15

GLA_H = 4
GLA_DK = 128
GLA_DV = 256
GLA_QK = GLA_H * GLA_DK
GLA_V = GLA_H * GLA_DV
GLA_NORMALIZER = 16.0
GLA_C = 128

SSM_DI = 2048
SSM_P = 64
SSM_H = 32
SSM_G = 4
SSM_HPG = 8
SSM_N = 128
SSM_CONV_DIM = SSM_DI + 2 * SSM_G * SSM_N
SSM_C = 128
GD = SSM_HPG * SSM_P

DEC_SEQ = 4
PAST_LEN = 16384

VMEM_LIMIT = 60 * 1024 * 1024


def _dot(a, b):
    return jnp.dot(a, b, preferred_element_type=f32)


def _dot_nt(a, b):
    return lax.dot_general(a, b, (((1,), (1,)), ((), ())), preferred_element_type=f32)


def _dot_tn(a, b):
    return lax.dot_general(a, b, (((0,), (0,)), ((), ())), preferred_element_type=f32)


def _dot_hi(a, b):
    return jnp.dot(a, b, precision=lax.Precision.HIGHEST, preferred_element_type=f32)


def _split3(x):
    hi = x.astype(bf16)
    r1 = x - hi.astype(f32)
    mid = r1.astype(bf16)
    lo = (r1 - mid.astype(f32)).astype(bf16)
    return hi, mid, lo


def _sel_dot(sel, parts):
    hi, mid, lo = parts
    return _dot(sel, hi) + _dot(sel, mid) + _dot(sel, lo)


def _rms(x, g):
    return x * lax.rsqrt(jnp.mean(x * x, axis=-1, keepdims=True) + EPS) * g


def _sigmoid(x):
    return 0.5 + 0.5 * jnp.tanh(0.5 * x)


def _silu(x):
    hx = 0.5 * x
    return hx + hx * jnp.tanh(hx)


def _cspec(shape):
    nd = len(shape)
    return pl.BlockSpec(shape, lambda *_: (0,) * nd, pipeline_mode=pl.Buffered(1))


def _params(*sem):
    return pltpu.CompilerParams(dimension_semantics=tuple(sem), vmem_limit_bytes=VMEM_LIMIT)


def _iota2(shape, dim):
    return lax.broadcasted_iota(jnp.int32, shape, dim)


def _two_stage(step, bufs0, bufs1):
    par = lax.rem(pl.program_id(0), 2)

    @pl.when(par == 0)
    def _():
        step(bufs0, bufs1)

    @pl.when(par == 1)
    def _():
        step(bufs1, bufs0)


def _stage_specs(tm, nt):
    a = pl.BlockSpec((tm, D), lambda s: (jnp.minimum(s, nt - 1), 0))
    b = pl.BlockSpec((tm, D), lambda s: (jnp.clip(s - 1, 0, nt - 1), 0))
    return a, b


def _ffn_kernel(*refs, ple, final, n_prompt):
    xp_ref, xs_ref, g_ref, wg_ref, wu_ref, wd_ref = refs[:6]
    k = 6
    if ple:
        pp_ref, ps_ref, gp_ref, wpg_ref, wpp_ref = refs[k:k + 5]
        k += 5
    if final:
        gf_ref = refs[k]
        k += 1
    op_ref, os_ref, h_scr, acc_scr = refs[k:k + 4]

    def gate_up(j):
        sl = slice(j * FC, (j + 1) * FC)
        hb = h_scr[...]
        return _dot(hb, wg_ref[:, sl].astype(bf16)), _dot(hb, wu_ref[:, sl].astype(bf16))

    def tile(x_ref, p_ref, o_ref):
        x = x_ref[...]
        h_scr[...] = _rms(x, g_ref[...]).astype(bf16)
        gu = gate_up(0)
        pend = []
        for j in range(NF):
            pend.append((_silu(gu[0]) * gu[1]).astype(bf16))
            if j + 1 < NF:
                gu = gate_up(j + 1)
            if len(pend) == 2 or j + 1 == NF:
                j0 = j + 1 - len(pend)
                a = pend[0] if len(pend) == 1 else jnp.concatenate(pend, axis=1)
                contrib = _dot(a, wd_ref[j0 * FC:(j + 1) * FC, :].astype(bf16))
                if j0 == 0:
                    acc_scr[...] = contrib
                else:
                    acc_scr[...] += contrib
                pend = []
        y = x + 0.5 * acc_scr[...]
        if ple:
            h2 = _rms(y, gp_ref[...]).astype(bf16)
            gate = _sigmoid(_dot(h2, wpg_ref[...]))
            y = y + gate * _dot(p_ref[...].astype(bf16), wpp_ref[...])
        if final:
            y = _rms(y, gf_ref[...])
        o_ref[...] = y

    is_prompt = pl.program_id(0) < n_prompt

    @pl.when(is_prompt)
    def _():
        tile(xp_ref, pp_ref if ple else None, op_ref)

    @pl.when(jnp.logical_not(is_prompt))
    def _():
        tile(xs_ref, ps_ref if ple else None, os_ref)


def _lspec(shape, layer):
    nd = len(shape)
    return pl.BlockSpec((None,) + tuple(shape), lambda *_: (layer,) + (0,) * nd,
                        pipeline_mode=pl.Buffered(1))


def _ffn(xp, xs, layer, g, wg, wu, wd, ple=None, final_g=None):
    tm = xs.shape[0]
    n_prompt = xp.shape[0] // tm
    prow = lambda w: pl.BlockSpec((tm, w), lambda i: (jnp.minimum(i, n_prompt - 1), 0))
    srow = lambda w: pl.BlockSpec((tm, w), lambda i: (0, 0), pipeline_mode=pl.Buffered(1))
    in_specs = [prow(D), srow(D), _lspec((1, D), layer), _lspec((D, F), layer), _lspec((D, F), layer),
                _lspec((F, D), layer)]
    args = [xp, xs, g, wg, wu, wd]
    if ple is not None:
        pp, ps, gp, wpg, wpp = ple
        in_specs += [pl.BlockSpec((None, tm, D_PLE), lambda i: (layer, jnp.minimum(i, n_prompt - 1), 0)),
                     pl.BlockSpec((None, tm, D_PLE), lambda i: (layer, 0, 0), pipeline_mode=pl.Buffered(1)),
                     _lspec((1, D), layer), _lspec((D, D), layer), _lspec((D_PLE, D), layer)]
        args += [pp, ps, gp, wpg, wpp]
    if final_g is not None:
        in_specs.append(_cspec((1, D)))
        args.append(final_g.reshape(1, D))
    return pl.pallas_call(
        functools.partial(_ffn_kernel, ple=ple is not None, final=final_g is not None, n_prompt=n_prompt),
        out_shape=(jax.ShapeDtypeStruct(xp.shape, f32), jax.ShapeDtypeStruct(xs.shape, f32)),
        grid=(n_prompt + 1,),
        in_specs=in_specs,
        out_specs=(prow(D), pl.BlockSpec((tm, D), lambda i: (0, 0))),
        scratch_shapes=[pltpu.VMEM((tm, D), bf16), pltpu.VMEM((tm, D), f32)],
        compiler_params=_params("arbitrary"),
        name="ffn",
    )(*args)


def _gelu(x):
    c = 0.7978845608028654
    hx = 0.5 * x
    return hx + hx * jnp.tanh(x * (c + (c * 0.044715) * (x * x)))


def _gmlp_act(y, ln_ref):
    y = _gelu(y)
    u = y[:, :D]
    v = y[:, D:]
    vc = v - jnp.mean(v, axis=-1, keepdims=True)
    v = vc * lax.rsqrt(jnp.mean(vc * vc, axis=-1, keepdims=True) + EPS) * ln_ref[...]
    return u, v


def _gmlp_pre(x, g, win_ref, ln_ref):
    h = _rms(x, g).astype(bf16)
    return _gmlp_act(_dot(h, win_ref[...]), ln_ref)


def _gmlp_prompt_kernel(x_ref, g_ref, win_ref, ln_ref, ws_ref, bst_ref, wout_ref, o_ref,
                        u_scr, vb_scr, t_scr):
    tm = x_ref.shape[0]
    x = x_ref[...]
    t_scr[...] = _rms(x, g_ref[...]).astype(bf16)
    cbw = D // 2
    gelu_block = lambda c0: _gelu(_dot(t_scr[...], win_ref[:, c0:c0 + cbw]))
    v = jnp.concatenate([gelu_block(D + i * cbw) for i in range(2)], axis=1)
    vc = v - jnp.mean(v, axis=-1, keepdims=True)
    v = vc * lax.rsqrt(jnp.mean(vc * vc, axis=-1, keepdims=True) + EPS) * ln_ref[...]
    vb_scr[...] = v.astype(bf16)
    for i in range(2):
        u_scr[:, i * cbw:(i + 1) * cbw] = gelu_block(i * cbw)
    causal = _iota2((GM_CHUNK, GM_CHUNK), 0) >= _iota2((GM_CHUNK, GM_CHUNK), 1)
    for g in range(GM_GROUPS):
        wm = jnp.where(causal, ws_ref[g], 0.0).astype(bf16)
        bias = bst_ref[:, g:g + 1]
        cs = slice(g * 128, (g + 1) * 128)
        for ch in range(tm // GM_CHUNK):
            rs = slice(ch * GM_CHUNK, (ch + 1) * GM_CHUNK)
            sv = _dot(wm, vb_scr[rs, cs]) + bias
            t_scr[rs, cs] = (u_scr[rs, cs] * sv).astype(bf16)
    o_ref[...] = x + _dot(t_scr[...], wout_ref[...])


def _gmlp_prompt(x, g, win, ln, ws, bst, wout, tm=512):
    T = x.shape[0]
    row = pl.BlockSpec((tm, D), lambda i: (i, 0))
    return pl.pallas_call(
        _gmlp_prompt_kernel,
        out_shape=jax.ShapeDtypeStruct((T, D), f32),
        grid=(T // tm,),
        in_specs=[row, _cspec((1, D)), _cspec((D, 2 * D)), _cspec((1, D)),
                  _cspec((GM_GROUPS, GM_CHUNK, GM_CHUNK)), _cspec((GM_CHUNK, GM_GROUPS)),
                  _cspec((D, D))],
        out_specs=row,
        scratch_shapes=[pltpu.VMEM((tm, D), f32), pltpu.VMEM((tm, D), bf16), pltpu.VMEM((tm, D), bf16)],
        compiler_params=_params("arbitrary"),
        name="gmlp_prompt",
    )(x, g.reshape(1, D), win, ln.reshape(1, D), ws, bst, wout)


def _gmlp_sample_kernel(x_ref, g_ref, win_ref, ln_ref, wj_ref, bt_ref, wout_ref, o_ref, v_ref):
    rows = x_ref.shape[0]
    n = rows // 8
    x = x_ref[...]
    u, v = _gmlp_pre(x, g_ref[...], win_ref, ln_ref)
    v_ref[...] = v
    acc = v.reshape(n, 8, D) * wj_ref[0][None]
    for j in range(1, DEC_SEQ):
        acc = acc + pltpu.roll(v, j, axis=0).reshape(n, 8, D) * wj_ref[j][None]
    sv = (acc + bt_ref[...][None]).reshape(rows, D)
    o_ref[...] = x + _dot((u * sv).astype(bf16), wout_ref[...])


def _gmlp_sample(x, g, win, ln, wj, bt, wout):
    T = x.shape[0]
    full = pl.BlockSpec((T, D), lambda i: (0, 0))
    return pl.pallas_call(
        _gmlp_sample_kernel,
        out_shape=(jax.ShapeDtypeStruct((T, D), f32), jax.ShapeDtypeStruct((T, D), f32)),
        grid=(1,),
        in_specs=[full, _cspec((1, D)), _cspec((D, 2 * D)), _cspec((1, D)),
                  _cspec((DEC_SEQ, 8, D)), _cspec((8, D)), _cspec((D, D))],
        out_specs=(full, full),
        compiler_params=_params("arbitrary"),
        name="gmlp_sample",
    )(x, g.reshape(1, D), win, ln.reshape(1, D), wj, bt, wout)


def _pool_prompt_kernel(x_ref, g_ref, pw_ref, sc_ref, o_ref, pn_ref, ext_scr, *, nl):
    tm = x_ref.shape[0]
    l = pl.program_id(1)

    @pl.when(l == 0)
    def _():
        ext_scr[0:16, :] = jnp.zeros((16, D), f32)

    x = x_ref[...]
    ext_scr[16:16 + tm, :] = _rms(x, g_ref[...])
    pos = l * tm + _iota2((tm, 1), 0)
    for gi, w in enumerate(POOL_WINDOWS):
        cs = slice(gi * POOL_GD, (gi + 1) * POOL_GD)
        s = ext_scr[:, cs]
        k = 1
        while k < w:
            s = s + pltpu.roll(s, k, axis=0)
            k *= 2
        s = s[16:16 + tm, :]
        h = ext_scr[16:16 + tm, cs]
        cnt = jnp.minimum(pos + 1, w).astype(f32)
        diff = (s / cnt - h).astype(bf16)
        o_ref[:, cs] = x[:, cs] + _dot(diff, pw_ref[gi]) * sc_ref[:, cs]

    @pl.when(l == nl - 1)
    def _():
        pn_ref[0] = ext_scr[16 + tm - POOL_HIST:16 + tm, :]

    ext_scr[0:16, :] = ext_scr[tm:tm + 16, :]


def _pool_prompt(x, g, pw, sc, nb, tm=512):
    T = x.shape[0]
    nl = T // nb // tm
    row = pl.BlockSpec((tm, D), lambda b, l: (b * nl + l, 0))
    return pl.pallas_call(
        functools.partial(_pool_prompt_kernel, nl=nl),
        out_shape=(jax.ShapeDtypeStruct((T, D), f32), jax.ShapeDtypeStruct((nb, POOL_HIST, D), f32)),
        grid=(nb, nl),
        in_specs=[row, _cspec((1, D)), _cspec((4, POOL_GD, POOL_GD)), _cspec((1, D))],
        out_specs=(row, pl.BlockSpec((1, POOL_HIST, D), lambda b, l: (b, 0, 0))),
        scratch_shapes=[pltpu.VMEM((tm + 16, D), f32)],
        compiler_params=_params("arbitrary", "arbitrary"),
        name="pool_prompt",
    )(x, g.reshape(1, D), pw, sc.reshape(1, D))


_SB = 8


def _pool_sample_kernel(x_ref, hist_ref, g_ref, pw_ref, sc_ref, sel_ref, o_ref, pn_ref, hp_scr, *, pos0):
    rows = x_ref.shape[0]
    nseq = hist_ref.shape[0]
    x = x_ref[...]
    h = _rms(x, g_ref[...])
    hp_scr[...] = jnp.zeros_like(hp_scr)
    for s in range(nseq):
        hp_scr[16 * s:16 * s + POOL_HIST, :] = hist_ref[s]
        pn_ref[s, 0:POOL_HIST - DEC_SEQ, :] = hist_ref[s, DEC_SEQ:POOL_HIST, :]
        pn_ref[s, POOL_HIST - DEC_SEQ:POOL_HIST, :] = h[DEC_SEQ * s:DEC_SEQ * (s + 1), :]
    t = _iota2((rows, 1), 0) & (DEC_SEQ - 1)
    parts = _split3(hp_scr[...])
    for gi, w in enumerate(POOL_WINDOWS):
        cs = slice(gi * POOL_GD, (gi + 1) * POOL_GD)
        hc = h[:, cs]
        s = hc
        for j in range(1, min(w, DEC_SEQ)):
            s = s + jnp.where(t >= j, pltpu.roll(hc, j, axis=0), 0.0)
        s = s + _sel_dot(sel_ref[gi], tuple(p[:, cs] for p in parts))
        cnt = jnp.minimum(pos0 + t + 1, w).astype(f32)
        diff = (s / cnt - hc).astype(bf16)
        o_ref[:, cs] = x[:, cs] + _dot(diff, pw_ref[gi]) * sc_ref[:, cs]


def _pool_sample(x, hist, g, pw, sc, sel, pos0):
    T = x.shape[0]
    rb = _SB * DEC_SEQ
    row = pl.BlockSpec((rb, D), lambda i: (i, 0))
    seq = pl.BlockSpec((_SB, POOL_HIST, D), lambda i: (i, 0, 0))
    return pl.pallas_call(
        functools.partial(_pool_sample_kernel, pos0=pos0),
        out_shape=(jax.ShapeDtypeStruct((T, D), f32), jax.ShapeDtypeStruct(hist.shape, f32)),
        grid=(T // rb,),
        in_specs=[row, seq, _cspec((1, D)), _cspec((4, POOL_GD, POOL_GD)), _cspec((1, D)),
                  _cspec((4, rb, _SB * 16))],
        out_specs=(row, seq),
        scratch_shapes=[pltpu.VMEM((_SB * 16, D), f32)],
        compiler_params=_params("arbitrary"),
        name="pool_sample",
    )(x, hist, g.reshape(1, D), pw, sc.reshape(1, D), sel)


def _cumsum_rows(x, seg=None):
    n = x.shape[0]
    seg = n if seg is None else seg
    pos = _iota2((n, 1), 0) & (seg - 1)
    s = 1
    while s < seg:
        x = x + jnp.where(pos >= s, pltpu.roll(x, s, axis=0), 0.0)
        s *= 2
    return x


def _gla_pre_body(x, g_ref, win_ref, wa1_ref, wa2_ref, ba_ref, qkvr_ref, la_ref):
    h = _rms(x, g_ref[...]).astype(bf16)
    qkvr_ref[...] = _dot(h, win_ref[...])
    a1 = _dot(h, wa1_ref[...]).astype(bf16)
    z = _dot(a1, wa2_ref[...]) + ba_ref[...]
    la_ref[...] = jax.nn.log_sigmoid(z) / GLA_NORMALIZER


def _gla_post_body(o, r, x, gn_ref, wout_ref):
    parts = []
    for hd in range(GLA_H):
        vs = slice(hd * GLA_DV, (hd + 1) * GLA_DV)
        parts.append(_rms(o[:, vs], gn_ref[:, vs]))
    on = jnp.concatenate(parts, axis=1)
    gated = (on * _silu(r)).astype(bf16)
    return x + _dot(gated, wout_ref[...])


def _gla_pre_kernel(x_ref, g_ref, win_ref, wa1_ref, wa2_ref, ba_ref, qkvr_ref, la_ref):
    _gla_pre_body(x_ref[...], g_ref, win_ref, wa1_ref, wa2_ref, ba_ref, qkvr_ref, la_ref)


def _gla_pre(x, g, win, wa1, wa2, ba, tm=512):
    T = x.shape[0]
    W = 2 * GLA_QK + 2 * GLA_V
    return pl.pallas_call(
        _gla_pre_kernel,
        out_shape=(jax.ShapeDtypeStruct((T, W), f32), jax.ShapeDtypeStruct((T, GLA_QK), f32)),
        grid=(T // tm,),
        in_specs=[pl.BlockSpec((tm, D), lambda i: (i, 0)), _cspec((1, D)), _cspec((D, W)),
                  _cspec((D, 16)), _cspec((16, GLA_QK)), _cspec((1, GLA_QK))],
        out_specs=(pl.BlockSpec((tm, W), lambda i: (i, 0)), pl.BlockSpec((tm, GLA_QK), lambda i: (i, 0))),
        compiler_params=_params("arbitrary"),
        name="gla_pre",
    )(x, g.reshape(1, D), win, wa1, wa2, ba.reshape(1, GLA_QK))


def _gla_chunk(q, k, v, b, bl, anc, mask, segs, C, halves=False):
    qs = q * (GLA_DK ** -0.5)
    q_dec = (qs * jnp.exp(b)).astype(bf16)
    k_end = k * jnp.exp(bl - b)
    q_mid = (qs * jnp.exp(b - anc)).astype(bf16)
    k_mid = (k * jnp.exp(anc - b)).astype(bf16)
    ebl = jnp.exp(bl)
    vb = v.astype(bf16)
    rid = _iota2((C, 1), 0)
    if halves:
        hc = C // 2
        edge = b[hc - 1:hc, :]
        q_x = jnp.where(rid >= hc, qs * jnp.exp(b - edge), 0.0).astype(bf16)
        k_x = jnp.where(rid < hc, k * jnp.exp(edge - b), 0.0).astype(bf16)
        mask = mask & ((_iota2((C, C), 0) >= hc) == (_iota2((C, C), 1) >= hc))
    outs = []
    for hd in range(GLA_H):
        ds_ = slice(hd * GLA_DK, (hd + 1) * GLA_DK)
        vs = slice(hd * GLA_DV, (hd + 1) * GLA_DV)
        sc = jnp.where(mask, _dot_nt(q_mid[:, ds_], k_mid[:, ds_]), 0.0)
        if halves:
            sc = sc + _dot_nt(q_x[:, ds_], k_x[:, ds_])
        o = _dot(sc.astype(bf16), vb[:, vs])
        o_int = None
        for (lo, hi, s_in, s_out) in segs:
            oi = _dot(q_dec[:, ds_], s_in[hd].astype(bf16))
            o_int = oi if o_int is None else jnp.where(rid >= lo, oi, o_int)
        outs.append(o + o_int)
        for (lo, hi, s_in, s_out) in segs:
            if ebl.shape[0] == 1:
                dec = jnp.transpose(jnp.broadcast_to(ebl[:, ds_], (8, GLA_DK)))[:, 0:1]
            else:
                r0 = max(hi - 8, 0)
                dec = jnp.transpose(ebl[r0:r0 + 8, ds_])[:, hi - 1 - r0:hi - r0]
            ke = k_end[:, ds_]
            if len(segs) > 1:
                ke = jnp.where((rid >= lo) & (rid < hi), ke, 0.0)
            s_out[hd] = s_in[hd] * dec + _dot_tn(ke.astype(bf16), vb[:, vs])
    return jnp.concatenate(outs, axis=1)


def _gla_prompt_kernel(xa_ref, xb_ref, g_ref, win_ref, wa1_ref, wa2_ref, ba_ref, gn_ref, wout_ref,
                       out_ref, sn_ref, q0, l0, q1, l1, o_scr, s_scr, *, nl):
    tm = xa_ref.shape[0]
    C = GLA_C
    s = pl.program_id(0)
    tpos = lax.rem(s - 1, nl)

    @pl.when(s == 0)
    def _():
        q1[...] = jnp.zeros_like(q1)
        l1[...] = jnp.zeros_like(l1)

    @pl.when((s == 0) | (tpos == 0))
    def _():
        s_scr[...] = jnp.zeros_like(s_scr)

    def step(bw, br):
        qw, lw = bw
        qr, lr = br
        _gla_pre_body(xa_ref[...], g_ref, win_ref, wa1_ref, wa2_ref, ba_ref, qw, lw)
        tri = _iota2((C, C), 0) >= _iota2((C, C), 1)
        for c in range(tm // C):
            rs = slice(c * C, (c + 1) * C)
            b = _cumsum_rows(lr[rs, :])
            bl = b[C - 1:C, :]
            anc = jnp.where(_iota2((C, 1), 0) < C // 2, b[C // 4 - 1:C // 4, :], b[3 * C // 4 - 1:3 * C // 4, :])
            q = qr[rs, 0:GLA_QK]
            k = qr[rs, GLA_QK:2 * GLA_QK]
            v = qr[rs, 2 * GLA_QK:2 * GLA_QK + GLA_V]
            o_scr[rs, :] = _gla_chunk(q, k, v, b, bl, anc, tri, [(0, C, s_scr, s_scr)], C, halves=True)
        out_ref[...] = _gla_post_body(o_scr[...], qr[:, 2 * GLA_QK + GLA_V:], xb_ref[...], gn_ref, wout_ref)

    _two_stage(step, (q0, l0), (q1, l1))

    @pl.when(tpos == nl - 1)
    def _():
        sn_ref[0] = s_scr[...]


def _gla_prompt(x, g, win, wa1, wa2, ba, gn, wout, nb, tm=256):
    T = x.shape[0]
    nt = T // tm
    nl = nt // nb
    W = 2 * GLA_QK + 2 * GLA_V
    a, b = _stage_specs(tm, nt)
    return pl.pallas_call(
        functools.partial(_gla_prompt_kernel, nl=nl),
        out_shape=(jax.ShapeDtypeStruct((T, D), f32),
                   jax.ShapeDtypeStruct((nb, GLA_H, GLA_DK, GLA_DV), f32)),
        grid=(nt + 1,),
        in_specs=[a, b, _cspec((1, D)), _cspec((D, W)), _cspec((D, 16)), _cspec((16, GLA_QK)),
                  _cspec((1, GLA_QK)), _cspec((1, GLA_V)), _cspec((GLA_V, D))],
        out_specs=(b, pl.BlockSpec((1, GLA_H, GLA_DK, GLA_DV), lambda s: (jnp.maximum(s - 1, 0) // nl, 0, 0, 0))),
        scratch_shapes=[pltpu.VMEM((tm, W), f32), pltpu.VMEM((tm, GLA_QK), f32),
                        pltpu.VMEM((tm, W), f32), pltpu.VMEM((tm, GLA_QK), f32),
                        pltpu.VMEM((tm, GLA_V), f32), pltpu.VMEM((GLA_H, GLA_DK, GLA_DV), f32)],
        compiler_params=_params("arbitrary"),
        name="gla_prompt",
    )(x, x, g.reshape(1, D), win, wa1, wa2, ba.reshape(1, GLA_QK), gn.reshape(1, GLA_V), wout)


_GLA_SB = 16


def _gla_sample_kernel(qkv_ref, la_ref, s0_ref, o_ref, sn_ref, b_scr, bl_scr):
    rows = qkv_ref.shape[0]
    r = _iota2((rows, rows), 0)
    c = _iota2((rows, rows), 1)
    same = (r >> 2) == (c >> 2)
    la = la_ref[...]
    b_scr[...] = _dot_hi((same & (r >= c)).astype(f32), la)
    bl_scr[...] = _dot_hi(same.astype(f32), la)
    r8 = _iota2((8, 8), 0)
    c8 = _iota2((8, 8), 1)
    mask8 = ((r8 >> 2) == (c8 >> 2)) & (r8 >= c8)

    def body(p, carry):
        rs = pl.ds(pl.multiple_of(p * 8, 8), 8)
        q = qkv_ref[rs, 0:GLA_QK]
        k = qkv_ref[rs, GLA_QK:2 * GLA_QK]
        v = qkv_ref[rs, 2 * GLA_QK:2 * GLA_QK + GLA_V]
        segs = [(0, 4, s0_ref.at[2 * p], sn_ref.at[2 * p]),
                (4, 8, s0_ref.at[2 * p + 1], sn_ref.at[2 * p + 1])]
        o_ref[rs, :] = _gla_chunk(q, k, v, b_scr[rs, :], bl_scr[rs, :], 0.0, mask8, segs, 8)
        return carry

    lax.fori_loop(0, rows // 8, body, 0)


def _gla_sample(qkvr, la, s0):
    T = qkvr.shape[0]
    nb = s0.shape[0]
    rb = _GLA_SB * DEC_SEQ
    W = 2 * GLA_QK + GLA_V
    st = pl.BlockSpec((_GLA_SB, GLA_H, GLA_DK, GLA_DV), lambda i: (i, 0, 0, 0))
    return pl.pallas_call(
        _gla_sample_kernel,
        out_shape=(jax.ShapeDtypeStruct((T, GLA_V), f32),
                   jax.ShapeDtypeStruct((nb, GLA_H, GLA_DK, GLA_DV), f32)),
        grid=(T // rb,),
        in_specs=[pl.BlockSpec((rb, W), lambda i: (i, 0)), pl.BlockSpec((rb, GLA_QK), lambda i: (i, 0)), st],
        out_specs=(pl.BlockSpec((rb, GLA_V), lambda i: (i, 0)), st),
        scratch_shapes=[pltpu.VMEM((rb, GLA_QK), f32), pltpu.VMEM((rb, GLA_QK), f32)],
        compiler_params=_params("arbitrary"),
        name="gla_sample",
    )(qkvr, la, s0)


def _gla_post_kernel(o_ref, r_ref, x_ref, gn_ref, wout_ref, out_ref):
    out_ref[...] = _gla_post_body(o_ref[...], r_ref[...], x_ref[...], gn_ref, wout_ref)


def _gla_post(o, qkvr, x, gn, wout, tm=512):
    T = x.shape[0]
    row = pl.BlockSpec((tm, D), lambda i: (i, 0))
    return pl.pallas_call(
        _gla_post_kernel,
        out_shape=jax.ShapeDtypeStruct((T, D), f32),
        grid=(T // tm,),
        in_specs=[row, pl.BlockSpec((tm, GLA_V), lambda i: (i, 2)), row, _cspec((1, GLA_V)),
                  _cspec((GLA_V, D))],
        out_specs=row,
        compiler_params=_params("arbitrary"),
        name="gla_post",
    )(o, qkvr, x, gn.reshape(1, GLA_V), wout)


def _ssd_pre_kernel(x_ref, g_ref, win_ref, z_ref, xbc_ref, dt_ref):
    h = _rms(x_ref[...], g_ref[...]).astype(bf16)
    z_ref[...] = _dot(h, win_ref[:, 0:SSM_DI])
    xbc_ref[...] = _dot(h, win_ref[:, SSM_DI:SSM_DI + SSM_CONV_DIM])
    dt_ref[...] = _dot(h, win_ref[:, SSM_DI + SSM_CONV_DIM:])


def _ssd_pre(x, g, win, tm=512):
    T = x.shape[0]
    return pl.pallas_call(
        _ssd_pre_kernel,
        out_shape=(jax.ShapeDtypeStruct((T, SSM_DI), f32), jax.ShapeDtypeStruct((T, SSM_CONV_DIM), f32),
                   jax.ShapeDtypeStruct((T, SSM_H), f32)),
        grid=(T // tm,),
        in_specs=[pl.BlockSpec((tm, D), lambda i: (i, 0)), _cspec((1, D)), _cspec((D, win.shape[1]))],
        out_specs=(pl.BlockSpec((tm, SSM_DI), lambda i: (i, 0)),
                   pl.BlockSpec((tm, SSM_CONV_DIM), lambda i: (i, 0)),
                   pl.BlockSpec((tm, SSM_H), lambda i: (i, 0))),
        compiler_params=_params("arbitrary"),
        name="ssd_pre",
    )(x, g.reshape(1, D), win)


def _conv_silu(terms, cw_ref, cb_ref):
    acc = terms[0] * cw_ref[0:1, :]
    for j in range(1, 4):
        acc = acc + terms[j] * cw_ref[j:j + 1, :]
    return _silu(cb_ref[...] + acc)


def _ssd_chunk(act_ref, rs, dtc, a_row, dx_ref, y_ref, mask, segs, C):
    rid = _iota2((C, 1), 0)
    cum = _cumsum_rows(dtc * a_row, segs[0][1] - segs[0][0])
    cum_last = cum[segs[-1][1] - 1:segs[-1][1], :]
    for (lo, hi, _, _) in segs[-2::-1]:
        cum_last = jnp.where(rid < hi, cum[hi - 1:hi, :], cum_last)
    e_cum = jnp.exp(cum)
    wend = jnp.exp(cum_last - cum) * dtc
    cum_t = jnp.transpose(cum)
    dt_t = jnp.transpose(dtc)
    dec_t = jnp.exp(cum_t)
    lane_lo = _iota2((C, 2 * SSM_P), 1) < SSM_P
    for g in range(SSM_G):
        gs = slice(g * GD, (g + 1) * GD)
        bm = act_ref[rs, SSM_DI + g * SSM_N:SSM_DI + (g + 1) * SSM_N].astype(bf16)
        cm = act_ref[rs, SSM_DI + SSM_G * SSM_N + g * SSM_N:SSM_DI + SSM_G * SSM_N + (g + 1) * SSM_N].astype(bf16)
        cb = _dot_nt(cm, bm)
        y_int = None
        for (lo, hi, s_in, s_out) in segs:
            yi = _dot_nt(cm, s_in[gs, :].astype(bf16))
            y_int = yi if y_int is None else jnp.where(rid >= lo, yi, y_int)
        xw = []
        for pr in range(SSM_HPG // 2):
            h0 = g * SSM_HPG + 2 * pr
            cols = slice(g * GD + 2 * pr * SSM_P, g * GD + (2 * pr + 2) * SSM_P)
            xs = act_ref[rs, cols]
            e_pair = jnp.where(lane_lo, e_cum[:, h0:h0 + 1], e_cum[:, h0 + 1:h0 + 2])
            w_pair = jnp.where(lane_lo, wend[:, h0:h0 + 1], wend[:, h0 + 1:h0 + 2])
            acc = y_int[:, 2 * pr * SSM_P:(2 * pr + 2) * SSM_P] * e_pair + xs * dx_ref[:, cols]
            xp = xs.astype(bf16)
            for hh, keep in ((h0, lane_lo), (h0 + 1, ~lane_lo)):
                seg = cum[:, hh:hh + 1] - cum_t[hh:hh + 1, :]
                mix = cb * jnp.exp(jnp.where(mask, seg, -jnp.inf)) * dt_t[hh:hh + 1, :]
                acc = acc + _dot(mix.astype(bf16), jnp.where(keep, xp, jnp.zeros_like(xp)))
            y_ref[rs, cols] = acc
            xw.append(xs * w_pair)
        xw = jnp.concatenate(xw, axis=1)
        for (lo, hi, s_in, s_out) in segs:
            xws = xw if len(segs) == 1 else jnp.where((rid >= lo) & (rid < hi), xw, 0.0)
            upd = _dot_tn(xws.astype(bf16), bm)
            for r in range(SSM_HPG):
                hh = g * SSM_HPG + r
                hs = slice(hh * SSM_P, (hh + 1) * SSM_P)
                s_out[hs, :] = s_in[hs, :] * dec_t[hh:hh + 1, hi - 1:hi] + upd[r * SSM_P:(r + 1) * SSM_P, :]


def _ssd_post_body(y, z, x, gn_ref, wout_ref):
    y = y * _silu(z)
    parts = []
    for g in range(SSM_G):
        gs = slice(g * GD, (g + 1) * GD)
        parts.append(_rms(y[:, gs], gn_ref[:, gs]))
    yn = jnp.concatenate(parts, axis=1).astype(bf16)
    return x + _dot(yn, wout_ref[...])


def _ssd_prompt_kernel(xa_ref, xb_ref, g_ref, win_ref, wdt_ref, cw_ref, cb_ref, dtb_ref, alog_ref, dx_ref, gn_ref,
                       wout_ref, out_ref, sn_ref, cn_ref, z0, e0, d0, z1, e1, d1, carry_scr, act_scr, y_scr, s_scr,
                       *, nl):
    tm = xa_ref.shape[0]
    C = SSM_C
    s = pl.program_id(0)
    tpos = lax.rem(s - 1, nl)

    @pl.when(s == 0)
    def _():
        z1[...] = jnp.zeros_like(z1)
        e1[...] = jnp.zeros_like(e1)
        d1[...] = jnp.zeros_like(d1)

    @pl.when((s == 0) | (tpos == 0))
    def _():
        carry_scr[...] = jnp.zeros_like(carry_scr)
        s_scr[...] = jnp.zeros_like(s_scr)

    def step(bw, br):
        zw, ew, dw = bw
        zr, er, dr = br
        h = _rms(xa_ref[...], g_ref[...]).astype(bf16)
        zw[...] = _dot(h, win_ref[:, 0:SSM_DI])
        ew[8:8 + tm, :] = _dot(h, win_ref[:, SSM_DI:SSM_DI + SSM_CONV_DIM])
        dw[...] = _dot(h, wdt_ref[...])
        er[0:8, :] = carry_scr[...]
        e = er[...]
        acc = e * cw_ref[0:1, :]
        for j in range(1, 4):
            acc = pltpu.roll(acc, 1, axis=0) + e * cw_ref[j:j + 1, :]
        act_scr[...] = _silu(cb_ref[...] + acc[8:8 + tm, :])
        carry_scr[...] = er[tm:tm + 8, :]
        a_row = -jnp.exp(alog_ref[...])
        tri = _iota2((C, C), 0) >= _iota2((C, C), 1)
        for c in range(tm // C):
            rs = slice(c * C, (c + 1) * C)
            dtc = jax.nn.softplus(dr[rs, :] + dtb_ref[...])
            _ssd_chunk(act_scr, rs, dtc, a_row, dx_ref, y_scr, tri, [(0, C, s_scr, s_scr)], C)
        out_ref[...] = _ssd_post_body(y_scr[...], zr[...], xb_ref[...], gn_ref, wout_ref)

        @pl.when(tpos == nl - 1)
        def _():
            cn_ref[0] = er[8 + tm - 3:8 + tm, :]

    _two_stage(step, (z0, e0, d0), (z1, e1, d1))

    @pl.when(tpos == nl - 1)
    def _():
        sn_ref[0] = s_scr[...]


def _ssd_prompt(x, g, win, wdt, cw, cb, dtb, alog, dx, gn, wout, nb, tm=256):
    T = x.shape[0]
    nt = T // tm
    nl = nt // nb
    WIN = win.shape[1]
    a, b = _stage_specs(tm, nt)
    seq = lambda s: (jnp.maximum(s - 1, 0) // nl, 0, 0)
    bufs = [pltpu.VMEM((tm, SSM_DI), f32), pltpu.VMEM((tm + 8, SSM_CONV_DIM), f32), pltpu.VMEM((tm, 128), f32)]
    return pl.pallas_call(
        functools.partial(_ssd_prompt_kernel, nl=nl),
        out_shape=(jax.ShapeDtypeStruct((T, D), f32),
                   jax.ShapeDtypeStruct((nb, SSM_DI, SSM_N), f32),
                   jax.ShapeDtypeStruct((nb, 3, SSM_CONV_DIM), f32)),
        grid=(nt + 1,),
        in_specs=[a, b, _cspec((1, D)), _cspec((D, WIN)), _cspec((D, 128)),
                  _cspec((4, SSM_CONV_DIM)), _cspec((1, SSM_CONV_DIM)), _cspec((1, 128)),
                  _cspec((1, 128)), _cspec((1, SSM_DI)), _cspec((1, SSM_DI)), _cspec((SSM_DI, D))],
        out_specs=(b, pl.BlockSpec((1, SSM_DI, SSM_N), seq), pl.BlockSpec((1, 3, SSM_CONV_DIM), seq)),
        scratch_shapes=bufs + bufs + [pltpu.VMEM((8, SSM_CONV_DIM), f32), pltpu.VMEM((tm, SSM_CONV_DIM), f32),
                                      pltpu.VMEM((tm, SSM_DI), f32), pltpu.VMEM((SSM_DI, SSM_N), f32)],
        compiler_params=_params("arbitrary"),
        name="ssd_prompt",
    )(x, x, g.reshape(1, D), win, wdt, cw, cb, dtb, alog, dx, gn.reshape(1, SSM_DI), wout)


def _ssd_sample_kernel(xbc_ref, dtr_ref, cs_ref, s0_ref, cw_ref, cb_ref, dtb_ref, alog_ref, dx_ref,
                       sel_ref, y_ref, sn_ref, cn_ref, act_scr, dt_scr, cp_scr):
    rows = xbc_ref.shape[0]
    xb = xbc_ref[...]
    cp_scr[...] = jnp.zeros_like(cp_scr)
    for s in range(cs_ref.shape[0]):
        cp_scr[8 * s:8 * s + 3, :] = cs_ref[s]
        cn_ref[s] = xb[DEC_SEQ * s + 1:DEC_SEQ * (s + 1), :]
    t = _iota2((rows, 1), 0) & (DEC_SEQ - 1)
    parts = _split3(cp_scr[...])
    terms = []
    for j in range(4):
        sh = 3 - j
        term = xb if sh == 0 else jnp.where(t >= sh, pltpu.roll(xb, sh, axis=0), 0.0)
        if j < 3:
            term = term + _sel_dot(sel_ref[j], parts)
        terms.append(term)
    act_scr[...] = _conv_silu(terms, cw_ref, cb_ref)
    dt_scr[...] = jnp.zeros_like(dt_scr)
    dt_scr[:, 0:SSM_H] = jax.nn.softplus(dtr_ref[...] + dtb_ref[...])
    a_row = -jnp.exp(alog_ref[...])
    r8 = _iota2((8, 8), 0)
    c8 = _iota2((8, 8), 1)
    mask8 = ((r8 >> 2) == (c8 >> 2)) & (r8 >= c8)

    def body(p, carry):
        rs = pl.ds(pl.multiple_of(p * 8, 8), 8)
        segs = [(0, 4, s0_ref.at[2 * p], sn_ref.at[2 * p]),
                (4, 8, s0_ref.at[2 * p + 1], sn_ref.at[2 * p + 1])]
        _ssd_chunk(act_scr, rs, dt_scr[rs, :], a_row, dx_ref, y_ref, mask8, segs, 8)
        return carry

    lax.fori_loop(0, rows // 8, body, 0)


def _ssd_sample(xbc, dtr, cs, s0, cw, cb, dtb, alog, dx, sel):
    T = xbc.shape[0]
    nb = s0.shape[0]
    rb = _SB * DEC_SEQ
    st = pl.BlockSpec((_SB, SSM_DI, SSM_N), lambda i: (i, 0, 0))
    cst = pl.BlockSpec((_SB, 3, SSM_CONV_DIM), lambda i: (i, 0, 0))
    return pl.pallas_call(
        _ssd_sample_kernel,
        out_shape=(jax.ShapeDtypeStruct((T, SSM_DI), f32), jax.ShapeDtypeStruct((nb, SSM_DI, SSM_N), f32),
                   jax.ShapeDtypeStruct(cs.shape, f32)),
        grid=(T // rb,),
        in_specs=[pl.BlockSpec((rb, SSM_CONV_DIM), lambda i: (i, 0)), pl.BlockSpec((rb, SSM_H), lambda i: (i, 0)),
                  cst, st,
                  _cspec((4, SSM_CONV_DIM)), _cspec((1, SSM_CONV_DIM)), _cspec((1, SSM_H)),
                  _cspec((1, 128)), _cspec((1, SSM_DI)), _cspec((3, rb, _SB * 8))],
        out_specs=(pl.BlockSpec((rb, SSM_DI), lambda i: (i, 0)), st, cst),
        scratch_shapes=[pltpu.VMEM((rb, SSM_CONV_DIM), f32), pltpu.VMEM((rb, 128), f32),
                        pltpu.VMEM((_SB * 8, SSM_CONV_DIM), f32)],
        compiler_params=_params("arbitrary"),
        name="ssd_sample",
    )(xbc, dtr, cs, s0, cw, cb, dtb, alog, dx, sel)


def _ssd_post_kernel(y_ref, z_ref, x_ref, gn_ref, wout_ref, out_ref):
    out_ref[...] = _ssd_post_body(y_ref[...], z_ref[...], x_ref[...], gn_ref, wout_ref)


def _ssd_post(y, z, x, gn, wout, tm=512):
    T = x.shape[0]
    row = pl.BlockSpec((tm, D), lambda i: (i, 0))
    wide = pl.BlockSpec((tm, SSM_DI), lambda i: (i, 0))
    return pl.pallas_call(
        _ssd_post_kernel,
        out_shape=jax.ShapeDtypeStruct((T, D), f32),
        grid=(T // tm,),
        in_specs=[wide, wide, row, _cspec((1, SSM_DI)), _cspec((SSM_DI, D))],
        out_specs=row,
        compiler_params=_params("arbitrary"),
        name="ssd_post",
    )(y, z, x, gn.reshape(1, SSM_DI), wout)


def _pool_selectors():
    sel = np.zeros((4, _SB * DEC_SEQ, _SB * 16), np.float32)
    for gi, w in enumerate(POOL_WINDOWS):
        for s in range(_SB):
            for t in range(DEC_SEQ):
                for i in range(POOL_HIST):
                    if i >= POOL_HIST + 1 + t - w:
                        sel[gi, s * DEC_SEQ + t, s * 16 + i] = 1.0
    return sel


def _conv_selectors():
    sel = np.zeros((3, _SB * DEC_SEQ, _SB * 8), np.float32)
    for j in range(3):
        for s in range(_SB):
            for t in range(DEC_SEQ):
                if t + j < 3:
                    sel[j, s * DEC_SEQ + t, s * 8 + t + j] = 1.0
    return sel


def kernel(x_prompt, x_sample, state_pool_l1, state_gla_l2, state_ssm_l3, state_conv_l3, p_prompt, p_sample, norm_ffn1, ffn1_gate, ffn1_up, ffn1_down, norm_mix, norm_ffn2, ffn2_gate, ffn2_up, ffn2_down, norm_ple, ple_gate, ple_proj, norm_final, gm_w_in, gm_ln, gm_w_s, gm_b_s, gm_w_out, pool_w, pool_scale, gla_w_in, gla_w_a1, gla_w_a2, gla_b_a, gla_norm, gla_w_out, ssm_w_in, ssm_conv_w, ssm_conv_b, ssm_dt_bias, ssm_a_log, ssm_d, ssm_norm, ssm_w_out):
    nbp, seq, _ = x_prompt.shape
    nbs, dseq, _ = x_sample.shape
    assert dseq == DEC_SEQ and seq % 512 == 0 and nbs % _GLA_SB == 0
    cast = lambda w: w.astype(bf16)

    w1g, w1u, w1d = ffn1_gate, ffn1_up, ffn1_down
    w2g, w2u, w2d = ffn2_gate, ffn2_up, ffn2_down
    wpg, wpp = cast(ple_gate), cast(ple_proj)
    gm_in, gm_out = cast(gm_w_in), cast(gm_w_out)
    pw = cast(pool_w)
    gla_in, gla_a1, gla_a2, gla_out = cast(gla_w_in), cast(gla_w_a1), cast(gla_w_a2), cast(gla_w_out)
    ssm_in, ssm_out = cast(ssm_w_in), cast(ssm_w_out)
    lane_pad = lambda v: jnp.pad(v, [(0, 0)] * (v.ndim - 1) + [(0, 128 - SSM_H)])
    ssm_wdt = lane_pad(cast(ssm_w_in[:, SSM_DI + SSM_CONV_DIM:]))
    n1g, n2g, npg = (n.reshape(DEPTH, 1, D) for n in (norm_ffn1, norm_ffn2, norm_ple))

    dx = jnp.repeat(ssm_d, SSM_P).reshape(1, SSM_DI)
    dtb = ssm_dt_bias.reshape(1, SSM_H)
    alog = ssm_a_log.reshape(1, SSM_H)
    cbias = ssm_conv_b.reshape(1, SSM_CONV_DIM)
    bst = jnp.transpose(gm_b_s)

    t8 = np.arange(8) % DEC_SEQ
    wj = []
    for j in range(DEC_SEQ):
        src = t8 - j
        vals = gm_w_s[:, t8, np.maximum(src, 0)]
        vals = jnp.where(jnp.asarray(src >= 0)[None, :], vals, 0.0)
        wj.append(jnp.repeat(jnp.transpose(vals), 128, axis=1))
    wj = jnp.stack(wj)
    bt = jnp.repeat(jnp.transpose(gm_b_s[:, t8]), 128, axis=1)

    pool_sel = jnp.asarray(_pool_selectors(), bf16)
    conv_sel = jnp.asarray(_conv_selectors(), bf16)

    xp = x_prompt.reshape(nbp * seq, D)
    xs = x_sample.reshape(nbs * dseq, D)
    pp = p_prompt.reshape(DEPTH, nbp * seq, D_PLE)
    ps = p_sample.reshape(DEPTH, nbs * dseq, D_PLE)
    ssm_s0 = state_ssm_l3.reshape(nbs, SSM_DI, SSM_N)

    outs = {}
    for i in range(DEPTH):
        xp, xs = _ffn(xp, xs, i, n1g, w1g, w1u, w1d)
        gmix = norm_mix[i]
        if i == 0:
            xp = _gmlp_prompt(xp, gmix, gm_in, gm_ln, gm_w_s, bst, gm_out)
            xs, chunk_v = _gmlp_sample(xs, gmix, gm_in, gm_ln, wj, bt, gm_out)
            outs["chunk_v"] = chunk_v.reshape(nbs, dseq, D)
        elif i == 1:
            xp, outs["pool_p"] = _pool_prompt(xp, gmix, pw, pool_scale, nbp)
            xs, outs["pool_s"] = _pool_sample(xs, state_pool_l1, gmix, pw, pool_scale, pool_sel, PAST_LEN)
        elif i == 2:
            xp, outs["gla_p"] = _gla_prompt(xp, gmix, gla_in, gla_a1, gla_a2, gla_b_a, gla_norm, gla_out, nbp)
            qkvr, la = _gla_pre(xs, gmix, gla_in, gla_a1, gla_a2, gla_b_a)
            o, outs["gla_s"] = _gla_sample(qkvr, la, state_gla_l2)
            xs = _gla_post(o, qkvr, xs, gla_norm, gla_out)
        else:
            xp, sn, outs["conv_p"] = _ssd_prompt(xp, gmix, ssm_in, ssm_wdt, ssm_conv_w, cbias, lane_pad(dtb),
                                                 lane_pad(alog), dx, ssm_norm, ssm_out, nbp)
            outs["ssm_p"] = sn.reshape(nbp, SSM_H, SSM_P, SSM_N)
            z, xbc, dtr = _ssd_pre(xs, gmix, ssm_in)
            y, sn, outs["conv_s"] = _ssd_sample(xbc, dtr, state_conv_l3, ssm_s0, ssm_conv_w, cbias, dtb,
                                                lane_pad(alog), dx, conv_sel)
            outs["ssm_s"] = sn.reshape(nbs, SSM_H, SSM_P, SSM_N)
            xs = _ssd_post(y, z, xs, ssm_norm, ssm_out)
        fin = norm_final if i == DEPTH - 1 else None
        xp, xs = _ffn(xp, xs, i, n2g, w2g, w2u, w2d, ple=(pp, ps, npg, wpg, wpp), final_g=fin)

    return (xp.reshape(nbp, seq, D), xs.reshape(nbs, dseq, D), outs["chunk_v"], outs["pool_p"], outs["pool_s"],
            outs["gla_p"], outs["gla_s"], outs["ssm_p"], outs["ssm_s"], outs["conv_p"], outs["conv_s"])
```
